```python
import jax, jax.numpy as jnp
from jax import lax
import numpy as np

D_MODEL = 1024
BATCH = 8
SEQ = 2048
DEPTH = 1
DEC_BATCH = 128
DEC_SEQ = 4
PAST_LEN = 16384
PAGE_SIZE = 128

D_MIX = D_MODEL
D_POOL = D_MIX // 2
POOL_WINDOWS = (2, 4, 8, 16)
POOL_GROUP = D_POOL // len(POOL_WINDOWS)
POOL_STATE = max(POOL_WINDOWS) - 1
D_CONV = D_MIX - D_POOL
CONV_WIDTH = 31
CONV_STATE = CONV_WIDTH - 1
D_IN = D_POOL + 2 * D_CONV
D_PLE = 256
N_KEYS = 128
N_EXPERTS = N_KEYS * N_KEYS
PEER_HEADS = 8
PEER_TOPK = 16
D_KEY = 256
D_HALF = D_KEY // 2
TOK_BLOCK = 128
EPS = 1e-6

kernel_name = 'hymba_pool_conformer_peer_decoder_step'


def rmsnorm(x, g):
    xf = x.astype(jnp.float32)
    y = xf * lax.rsqrt(jnp.mean(xf * xf, axis=-1, keepdims=True) + EPS)
    return (y * g.astype(jnp.float32)).astype(x.dtype)


def layernorm(x, g, b):
    xf = x.astype(jnp.float32)
    mu = jnp.mean(xf, axis=-1, keepdims=True)
    var = jnp.mean(jnp.square(xf - mu), axis=-1, keepdims=True)
    y = (xf - mu) * lax.rsqrt(var + EPS)
    return (y * g.astype(jnp.float32) + b.astype(jnp.float32)).astype(x.dtype)


def pool_mixer(a, prefix, pos0, w_pool, pool_scale):
    T = a.shape[1]
    full = jnp.concatenate([prefix, a], axis=1)
    c = jnp.cumsum(full.astype(jnp.float32), axis=1)
    c = jnp.pad(c, ((0, 0), (1, 0), (0, 0)))
    pos = pos0 + jnp.arange(T)
    P = POOL_STATE
    outs = []
    for g, w in enumerate(POOL_WINDOWS):
        sl = slice(g * POOL_GROUP, (g + 1) * POOL_GROUP)
        win_sum = c[:, P + 1:P + 1 + T, sl] - c[:, P + 1 - w:P + 1 - w + T, sl]
        cnt = jnp.minimum(pos + 1, w).astype(jnp.float32)[None, :, None]
        y = win_sum / cnt - a[:, :, sl].astype(jnp.float32)
        outs.append(jnp.einsum('btc,cd->btd', y.astype(a.dtype), w_pool[g]))
    y = jnp.concatenate(outs, axis=-1) * pool_scale
    return y, full[:, -POOL_STATE:]


def conv_module(z, prefix, w_dw, b_dw, ln_g, ln_b):
    val, gt = z[..., :D_CONV], z[..., D_CONV:]
    u = val * jax.nn.sigmoid(gt)
    full = jnp.concatenate([prefix, u], axis=1)
    y = lax.conv_general_dilated(full, w_dw[:, None, :], window_strides=(1,), padding='VALID',
                                 dimension_numbers=('NWC', 'WIO', 'NWC'),
                                 feature_group_count=D_CONV) + b_dw
    y = jax.nn.silu(layernorm(y, ln_g, ln_b))
    return y, full[:, -CONV_STATE:]


def peer(x, w_q, sub_keys, u_tab, v_tab):
    B, T, D = x.shape
    n = B * T
    xt = jnp.pad(x.reshape(n, D), ((0, (-n) % TOK_BLOCK), (0, 0)))
    xb = xt.reshape(-1, TOK_BLOCK, D)

    def block(xk):
        q = (xk @ w_q).reshape(TOK_BLOCK, PEER_HEADS, 2, D_HALF)
        s = jnp.einsum('thpc,hpkc->thpk', q, sub_keys).astype(jnp.float32)
        sv, si = lax.top_k(s, PEER_TOPK)
        cand = (sv[:, :, 0, :, None] + sv[:, :, 1, None, :]).reshape(TOK_BLOCK, PEER_HEADS, -1)
        cand_idx = (si[:, :, 0, :, None] * N_KEYS + si[:, :, 1, None, :]).reshape(TOK_BLOCK, PEER_HEADS, -1)
        top_s, top_j = lax.top_k(cand, PEER_TOPK)
        eidx = jnp.take_along_axis(cand_idx, top_j, axis=-1)
        gate = jax.nn.softmax(top_s, axis=-1)
        u = u_tab[eidx]
        v = v_tab[eidx]
        h = jax.nn.gelu(jnp.einsum('td,thkd->thk', xk, u).astype(jnp.float32))
        coef = (gate * h).astype(xk.dtype)
        return jnp.einsum('thk,thkd->td', coef, v)

    out = lax.map(block, xb).reshape(-1, D)[:n]
    return out.reshape(B, T, D)


def layer(h, p, pool_prefix, conv_prefix, pos0, g_mix, w_in, w_pool, pool_scale, w_dw, b_dw, ln_g, ln_b,
          w_out, g_ffn, w_q, sub_keys, u_tab, v_tab, g_ple, w_ple_gate, w_ple_proj):
    xn = rmsnorm(h, g_mix)
    z = jnp.einsum('btd,de->bte', xn, w_in)
    ya, new_pool = pool_mixer(z[..., :D_POOL], pool_prefix, pos0, w_pool, pool_scale)
    yb, new_conv = conv_module(z[..., D_POOL:], conv_prefix, w_dw, b_dw, ln_g, ln_b)
    h = h + jnp.einsum('btc,cd->btd', jnp.concatenate([ya, yb], axis=-1), w_out)
    h = h + peer(rmsnorm(h, g_ffn), w_q, sub_keys, u_tab, v_tab)
    gate = jax.nn.sigmoid(jnp.einsum('btd,de->bte', rmsnorm(h, g_ple), w_ple_gate))
    h = h + jnp.einsum('btp,pd->btd', p, w_ple_proj) * gate
    return h, new_pool, new_conv


def setup_inputs(seed: int = 0) -> dict:
    key = jax.random.key(seed)
    ks = jax.random.split(key, 32)
    nrm = lambda k, shape, s: jax.random.normal(k, shape, jnp.float32) * s
    gain = lambda k, shape: 1.0 + 0.05 * jax.random.normal(k, shape, jnp.float32)
    return {
        'x_prompt': nrm(ks[0], (BATCH, SEQ, D_MODEL), 1.0),
        'x_sample': nrm(ks[1], (DEC_BATCH, DEC_SEQ, D_MODEL), 1.0),
        'state_pool': nrm(ks[2], (DEPTH, DEC_BATCH, POOL_STATE, D_POOL), 1.0),
        'state_conv': nrm(ks[3], (DEPTH, DEC_BATCH, CONV_STATE, D_CONV), 0.5),
        'p_prompt': nrm(ks[4], (DEPTH, BATCH, SEQ, D_PLE), 1.0),
        'p_sample': nrm(ks[5], (DEPTH, DEC_BATCH, DEC_SEQ, D_PLE), 1.0),
        'g_mix': gain(ks[6], (DEPTH, D_MODEL)),
        'w_in': nrm(ks[7], (DEPTH, D_MODEL, D_IN), D_MODEL ** -0.5),
        'w_pool': nrm(ks[8], (DEPTH, len(POOL_WINDOWS), POOL_GROUP, POOL_GROUP), POOL_GROUP ** -0.5),
        'pool_scale': gain(ks[9], (DEPTH, D_POOL)),
        'w_dw': nrm(ks[10], (DEPTH, CONV_WIDTH, D_CONV), CONV_WIDTH ** -0.5),
        'b_dw': nrm(ks[11], (DEPTH, D_CONV), 0.02),
        'ln_g': gain(ks[12], (DEPTH, D_CONV)),
        'ln_b': nrm(ks[13], (DEPTH, D_CONV), 0.02),
        'w_out': nrm(ks[14], (DEPTH, D_MIX, D_MODEL), D_MIX ** -0.5),
        'g_ffn': gain(ks[15], (DEPTH, D_MODEL)),
        'w_q': nrm(ks[16], (DEPTH, D_MODEL, PEER_HEADS * D_KEY), D_MODEL ** -0.5),
        'sub_keys': nrm(ks[17], (DEPTH, PEER_HEADS, 2, N_KEYS, D_HALF), D_HALF ** -0.5),
        'u_tab': nrm(ks[18], (DEPTH, N_EXPERTS, D_MODEL), D_MODEL ** -0.5),
        'v_tab': nrm(ks[19], (DEPTH, N_EXPERTS, D_MODEL), 0.5),
        'g_ple': gain(ks[20], (DEPTH, D_MODEL)),
        'w_ple_gate': nrm(ks[21], (DEPTH, D_MODEL, D_MODEL), D_MODEL ** -0.5),
        'w_ple_proj': nrm(ks[22], (DEPTH, D_PLE, D_MODEL), D_PLE ** -0.5),
        'g_final': gain(ks[23], (D_MODEL,)),
    }


def reference(x_prompt, x_sample, state_pool, state_conv, p_prompt, p_sample, g_mix, w_in, w_pool, pool_scale,
              w_dw, b_dw, ln_g, ln_b, w_out, g_ffn, w_q, sub_keys, u_tab, v_tab, g_ple, w_ple_gate, w_ple_proj,
              g_final):
    hp, hs = x_prompt, x_sample
    B = x_prompt.shape[0]
    pool_p, conv_p, pool_s, conv_s = [], [], [], []
    for i in range(DEPTH):
        params = (g_mix[i], w_in[i], w_pool[i], pool_scale[i], w_dw[i], b_dw[i], ln_g[i], ln_b[i], w_out[i],
                  g_ffn[i], w_q[i], sub_keys[i], u_tab[i], v_tab[i], g_ple[i], w_ple_gate[i], w_ple_proj[i])
        zp_pool = jnp.zeros((B, POOL_STATE, D_POOL), x_prompt.dtype)
        zp_conv = jnp.zeros((B, CONV_STATE, D_CONV), x_prompt.dtype)
        hp, npool, nconv = layer(hp, p_prompt[i], zp_pool, zp_conv, 0, *params)
        pool_p.append(npool)
        conv_p.append(nconv)
        hs, spool, sconv = layer(hs, p_sample[i], state_pool[i], state_conv[i], PAST_LEN, *params)
        pool_s.append(spool)
        conv_s.append(sconv)
    y_prompt = rmsnorm(hp, g_final)
    y_sample = rmsnorm(hs, g_final)
    return (y_prompt, y_sample, jnp.stack(pool_p), jnp.stack(conv_p), jnp.stack(pool_s), jnp.stack(conv_s))
```

```python
import functools

import jax
import jax.numpy as jnp
from jax import lax
from jax.experimental import pallas as pl
from jax.experimental.pallas import tpu as pltpu

EPS = 1e-6
PAST_LEN = 16384
POOL_WINDOWS = (2, 4, 8, 16)
TOPK = 16

LANES = 128
SUBLANES = 8
VMEM_LIMIT = 56 * 1024 * 1024

BF16 = jnp.bfloat16
F32 = jnp.float32


def _dot(a, b):
    return jnp.dot(a, b, preferred_element_type=F32)


def _rmsnorm(x, g):
    return x * lax.rsqrt(jnp.mean(x * x, axis=-1, keepdims=True) + EPS) * g


def _sigmoid(x):
    return 1.0 / (1.0 + jnp.exp(-x))


def _gelu_tanh(x):
    return 0.5 * x * (1.0 + jnp.tanh(0.7978845608028654 * (x + 0.044715 * (x * x * x))))


def _conv_tail(y, bdw, lng, lnb):
    y = y + bdw
    mu = jnp.mean(y, axis=-1, keepdims=True)
    d = y - mu
    var = jnp.mean(d * d, axis=-1, keepdims=True)
    yn = d * lax.rsqrt(var + EPS) * lng + lnb
    return yn * _sigmoid(yn)


A_PAD = 16
U_PAD = 32


def _mixer_prompt_kernel(x_ref, gmix_ref, win_ref, wpool_ref, pscale_ref, wdw_ref, bdw_ref, lng_ref, lnb_ref,
                         wout_ref, h_ref, pool_out_ref, conv_out_ref, abuf, ubuf, *, T, d_pool, d_conv):
    s = pl.program_id(1)
    n_s = pl.num_programs(1)
    conv_width = wdw_ref.shape[0]
    pool_state = max(POOL_WINDOWS) - 1
    conv_state = conv_width - 1

    @pl.when(s == 0)
    def _():
        abuf[0:A_PAD, :] = jnp.zeros((A_PAD, d_pool), F32)
        ubuf[0:U_PAD, :] = jnp.zeros((U_PAD, d_conv), F32)

    x = x_ref[0]
    xn = _rmsnorm(x, gmix_ref[...]).astype(BF16)
    z = _dot(xn, win_ref[...])
    a = z[:, :d_pool]
    val = z[:, d_pool:d_pool + d_conv]
    gt = z[:, d_pool + d_conv:]
    u = val * _sigmoid(gt)
    abuf[A_PAD:A_PAD + T, :] = a
    ubuf[U_PAD:U_PAD + T, :] = u

    group = d_pool // len(POOL_WINDOWS)
    pos = s * T + lax.broadcasted_iota(jnp.int32, (T, group), 0)
    ys = []
    for g, w in enumerate(POOL_WINDOWS):
        lo = g * group
        win = abuf[A_PAD:A_PAD + T, lo:lo + group]
        for jj in range(1, w):
            win = win + abuf[A_PAD - jj:A_PAD - jj + T, lo:lo + group]
        cnt = jnp.minimum(pos + 1, w).astype(F32)
        ys.append(win / cnt - a[:, lo:lo + group])
    ypool = jnp.concatenate(ys, axis=-1).astype(BF16)
    ya = _dot(ypool, wpool_ref[...]) * pscale_ref[...]

    off = U_PAD - conv_state
    y = ubuf[off:off + T, :] * wdw_ref[0:1, :]
    for k in range(1, conv_width):
        y = y + ubuf[off + k:off + k + T, :] * wdw_ref[k:k + 1, :]
    yb = _conv_tail(y, bdw_ref[...], lng_ref[...], lnb_ref[...])

    cat = jnp.concatenate([ya, yb], axis=-1).astype(BF16)
    h_ref[0] = x + _dot(cat, wout_ref[...])

    @pl.when(s == n_s - 1)
    def _():
        pool_out_ref[0, 0] = abuf[A_PAD + T - pool_state:A_PAD + T, :]
        conv_out_ref[0, 0] = ubuf[U_PAD + T - conv_state:U_PAD + T, :]

    abuf[0:A_PAD, :] = abuf[T:T + A_PAD, :]
    ubuf[0:U_PAD, :] = ubuf[T:T + U_PAD, :]


def _mixer_prompt(x, gmix, win, wpool_bd, pscale, wdw, bdw, lng, lnb, wout, *, T):
    B, S, D = x.shape
    d_pool = wpool_bd.shape[0]
    d_conv = wdw.shape[1]
    pool_state = max(POOL_WINDOWS) - 1
    conv_state = wdw.shape[0] - 1
    const2 = lambda b, s: (0, 0)
    kern = functools.partial(_mixer_prompt_kernel, T=T, d_pool=d_pool, d_conv=d_conv)
    return pl.pallas_call(
        kern,
        grid=(B, S // T),
        in_specs=[
            pl.BlockSpec((1, T, D), lambda b, s: (b, s, 0)),
            pl.BlockSpec(gmix.shape, const2),
            pl.BlockSpec(win.shape, const2),
            pl.BlockSpec(wpool_bd.shape, const2),
            pl.BlockSpec(pscale.shape, const2),
            pl.BlockSpec(wdw.shape, const2),
            pl.BlockSpec(bdw.shape, const2),
            pl.BlockSpec(lng.shape, const2),
            pl.BlockSpec(lnb.shape, const2),
            pl.BlockSpec(wout.shape, const2),
        ],
        out_specs=[
            pl.BlockSpec((1, T, D), lambda b, s: (b, s, 0)),
            pl.BlockSpec((1, 1, pool_state, d_pool), lambda b, s: (0, b, 0, 0)),
            pl.BlockSpec((1, 1, conv_state, d_conv), lambda b, s: (0, b, 0, 0)),
        ],
        out_shape=[
            jax.ShapeDtypeStruct((B, S, D), F32),
            jax.ShapeDtypeStruct((1, B, pool_state, d_pool), F32),
            jax.ShapeDtypeStruct((1, B, conv_state, d_conv), F32),
        ],
        scratch_shapes=[
            pltpu.VMEM((T + A_PAD, d_pool), F32),
            pltpu.VMEM((T + U_PAD, d_conv), F32),
        ],
        compiler_params=pltpu.CompilerParams(
            dimension_semantics=("arbitrary", "arbitrary"), vmem_limit_bytes=VMEM_LIMIT),
        name="mixer_prompt",
    )(x, gmix, win, wpool_bd, pscale, wdw, bdw, lng, lnb, wout)


def _mixer_sample_kernel(x_ref, pool_ref, conv_ref, gmix_ref, win_ref, wpool_ref, pscale_ref, wdw_ref, bdw_ref,
                         lng_ref, lnb_ref, wout_ref, h_ref, pool_out_ref, conv_out_ref, *, d_pool, d_conv):
    Ts, BC, D = x_ref.shape
    P = pool_ref.shape[0]
    C = conv_ref.shape[0]
    conv_width = wdw_ref.shape[0]

    x = x_ref[...].reshape(Ts * BC, D)
    xn = _rmsnorm(x, gmix_ref[...]).astype(BF16)
    z = _dot(xn, win_ref[...])
    a = z[:, :d_pool]
    u = z[:, d_pool:d_pool + d_conv] * _sigmoid(z[:, d_pool + d_conv:])

    def full_a(k):
        return pool_ref[k] if k < P else a[(k - P) * BC:(k - P + 1) * BC, :]

    def full_u(k):
        return conv_ref[k] if k < C else u[(k - C) * BC:(k - C + 1) * BC, :]

    group = d_pool // len(POOL_WINDOWS)
    ya_rows, yb_rows = [], []
    for t in range(Ts):
        ys = []
        for g, w in enumerate(POOL_WINDOWS):
            lo = g * group
            win = full_a(P + t)[:, lo:lo + group]
            for jj in range(1, w):
                win = win + full_a(P + t - jj)[:, lo:lo + group]
            cnt = float(min(PAST_LEN + t + 1, w))
            ys.append(win / cnt - full_a(P + t)[:, lo:lo + group])
        ya_rows.append(jnp.concatenate(ys, axis=-1))
        y = full_u(t) * wdw_ref[0:1, :]
        for k in range(1, conv_width):
            y = y + full_u(t + k) * wdw_ref[k:k + 1, :]
        yb_rows.append(y)
    ypool = jnp.concatenate(ya_rows, axis=0).astype(BF16)
    ya = _dot(ypool, wpool_ref[...]) * pscale_ref[...]
    yb = _conv_tail(jnp.concatenate(yb_rows, axis=0), bdw_ref[...], lng_ref[...], lnb_ref[...])
    cat = jnp.concatenate([ya, yb], axis=-1).astype(BF16)
    h = x + _dot(cat, wout_ref[...])
    h_ref[...] = h.reshape(Ts, BC, D)
    for k in range(P):
        pool_out_ref[k] = full_a(Ts + k)
    for k in range(C):
        conv_out_ref[k] = full_u(Ts + k)


def _mixer_sample(x_tm, pool_tm, conv_tm, gmix, win, wpool_bd, pscale, wdw, bdw, lng, lnb, wout, *, BC):
    Ts, B, D = x_tm.shape
    P, _, d_pool = pool_tm.shape
    C, _, d_conv = conv_tm.shape
    const2 = lambda b: (0, 0)
    kern = functools.partial(_mixer_sample_kernel, d_pool=d_pool, d_conv=d_conv)
    return pl.pallas_call(
        kern,
        grid=(B // BC,),
        in_specs=[
            pl.BlockSpec((Ts, BC, D), lambda b: (0, b, 0)),
            pl.BlockSpec((P, BC, d_pool), lambda b: (0, b, 0)),
            pl.BlockSpec((C, BC, d_conv), lambda b: (0, b, 0)),
            pl.BlockSpec(gmix.shape, const2),
            pl.BlockSpec(win.shape, const2),
            pl.BlockSpec(wpool_bd.shape, const2),
            pl.BlockSpec(pscale.shape, const2),
            pl.BlockSpec(wdw.shape, const2),
            pl.BlockSpec(bdw.shape, const2),
            pl.BlockSpec(lng.shape, const2),
            pl.BlockSpec(lnb.shape, const2),
            pl.BlockSpec(wout.shape, const2),
        ],
        out_specs=[
            pl.BlockSpec((Ts, BC, D), lambda b: (0, b, 0)),
            pl.BlockSpec((P, BC, d_pool), lambda b: (0, b, 0)),
            pl.BlockSpec((C, BC, d_conv), lambda b: (0, b, 0)),
        ],
        out_shape=[
            jax.ShapeDtypeStruct((Ts, B, D), F32),
            jax.ShapeDtypeStruct((P, B, d_pool), F32),
            jax.ShapeDtypeStruct((C, B, d_conv), F32),
        ],
        compiler_params=pltpu.CompilerParams(
            dimension_semantics=("arbitrary",), vmem_limit_bytes=VMEM_LIMIT),
        name="mixer_sample",
    )(x_tm, pool_tm, conv_tm, gmix, win, wpool_bd, pscale, wdw, bdw, lng, lnb, wout)


def _batcher_pairs(n):
    pairs = []
    t = (n - 1).bit_length()
    p = 1 << (t - 1)
    while p > 0:
        q = 1 << (t - 1)
        r = 0
        d = p
        while d > 0:
            for i in range(n - d):
                if (i & p) == r:
                    pairs.append((i, i + d))
            d = q - p
            q >>= 1
            r = p
        p >>= 1
    return pairs


_SORT16 = _batcher_pairs(TOPK)


def _top16_sorted(s):
    n = s.shape[0] // SUBLANES
    v = [s[k * SUBLANES:(k + 1) * SUBLANES, :] for k in range(n)]
    for i, j in _SORT16:
        hi = jnp.maximum(v[i], v[j])
        lo = jnp.minimum(v[i], v[j])
        v[i], v[j] = hi, lo
    for sh in (4, 2, 1):
        b = [pltpu.roll(x, sh, 0) for x in v]
        v = [jnp.maximum(v[k], b[n - 1 - k]) for k in range(n)]
        st = n // 2
        while st >= 1:
            for k in range(n):
                if not (k & st):
                    hi = jnp.maximum(v[k], v[k + st])
                    lo = jnp.minimum(v[k], v[k + st])
                    v[k], v[k + st] = hi, lo
            st //= 2
    return v


def _kth_largest(cands, k):
    def tree_max(xs):
        xs = list(xs)
        while len(xs) > 1:
            nxt = [jnp.maximum(xs[i], xs[i + 1]) for i in range(0, len(xs) - 1, 2)]
            if len(xs) % 2:
                nxt.append(xs[-1])
            xs = nxt
        return xs[0]

    cur = list(cands)
    for _ in range(k - 1):
        m = tree_max(cur)
        cur = [jnp.where(c == m, -jnp.inf, c) for c in cur]
    return tree_max(cur)


def _peer_kernel(h_ref, gffn_ref, wqT_ref, keys_ref, u_ref, vT_ref, out_ref,
                 xnT_ref, accT_ref, s0_ref, s1_ref, e0_ref, e1_ref, thr_ref, m0_ref, m1_ref, zinv_ref, coef_ref,
                 *, T, E, TC, heads):
    j = pl.program_id(1)
    n_j = pl.num_programs(1)
    n_keys = keys_ref.shape[1]
    d_half = keys_ref.shape[2]
    n_tiles = T // LANES

    @pl.when(j == 0)
    def _prep():
        xn = _rmsnorm(h_ref[...], gffn_ref[...])
        xnT = xn.T.astype(BF16)
        xnT_ref[...] = xnT
        accT_ref[...] = jnp.zeros(accT_ref.shape, F32)
        for hd in range(heads):
            qT = _dot(wqT_ref[hd * 2 * d_half:(hd + 1) * 2 * d_half, :], xnT).astype(BF16)
            s0_ref[hd] = _dot(keys_ref[2 * hd], qT[:d_half, :])
            s1_ref[hd] = _dot(keys_ref[2 * hd + 1], qT[d_half:, :])

        sub = lax.broadcasted_iota(jnp.int32, (SUBLANES, LANES), 0)

        def tile_body(c, carry):
            sl = pl.ds(pl.multiple_of(c * LANES, LANES), LANES)
            sv0 = [jnp.zeros((SUBLANES, LANES), F32)] * TOPK
            sv1 = [jnp.zeros((SUBLANES, LANES), F32)] * TOPK
            for hd in range(heads):
                t0 = _top16_sorted(s0_ref[hd, :, sl])
                t1 = _top16_sorted(s1_ref[hd, :, sl])
                sv0 = [jnp.where(sub == hd, t0[k], sv0[k]) for k in range(TOPK)]
                sv1 = [jnp.where(sub == hd, t1[k], sv1[k]) for k in range(TOPK)]
            pq = [(p, q) for p in range(TOPK) for q in range(TOPK) if (p + 1) * (q + 1) <= TOPK]
            cands = [sv0[p] + sv1[q] for p, q in pq]
            thr = _kth_largest(cands, TOPK)
            m0 = sv0[0]
            m1 = sv1[0]
            ex0 = [jnp.exp(x - m0) for x in sv0]
            ex1 = [jnp.exp(x - m1) for x in sv1]
            z = jnp.zeros((SUBLANES, LANES), F32)
            for (p, q), cnd in zip(pq, cands):
                z = z + jnp.where(cnd >= thr, ex0[p] * ex1[q], 0.0)
            thr_ref[:, sl] = thr
            m0_ref[:, sl] = m0
            m1_ref[:, sl] = m1
            zinv_ref[:, sl] = 1.0 / z
            return carry

        lax.fori_loop(0, n_tiles, tile_body, 0)

        for hd in range(heads):
            m0row = m0_ref[hd:hd + 1, :]
            m1row = m1_ref[hd:hd + 1, :]
            zrow = zinv_ref[hd:hd + 1, :]
            e0_ref[hd] = jnp.exp(s0_ref[hd] - m0row) * zrow
            e1_ref[hd] = jnp.exp(s1_ref[hd] - m1row)

    base_i = j * (E // n_keys)
    for c in range(T // TC):
        cs = slice(c * TC, (c + 1) * TC)
        st = _dot(u_ref[...], xnT_ref[:, cs])
        for i in range(E // n_keys):
            row = pl.ds(base_i + i, 1)
            gsum = jnp.zeros((n_keys, TC), F32)
            for hd in range(heads):
                s0row = s0_ref[hd, row, cs]
                e0row = e0_ref[hd, row, cs]
                thrrow = thr_ref[hd:hd + 1, cs]
                mask = (s1_ref[hd, :, cs] + s0row) >= thrrow
                gsum = gsum + jnp.where(mask, e1_ref[hd, :, cs], 0.0) * e0row
            hval = _gelu_tanh(st[i * n_keys:(i + 1) * n_keys, :])
            coef_ref[i * n_keys:(i + 1) * n_keys, cs] = (hval * gsum).astype(BF16)
        accT_ref[:, cs] += _dot(vT_ref[...], coef_ref[:, cs])

    @pl.when(j == n_j - 1)
    def _fin():
        out_ref[...] = h_ref[...] + accT_ref[...].T


def _peer(h, gffn, wqT, keys, u_bf, vT_bf, *, T, E, TC):
    N, D = h.shape
    heads = keys.shape[0] // 2
    n_keys = keys.shape[1]
    n_exp = u_bf.shape[0]
    kern = functools.partial(_peer_kernel, T=T, E=E, TC=TC, heads=heads)
    return pl.pallas_call(
        kern,
        grid=(N // T, n_exp // E),
        in_specs=[
            pl.BlockSpec((T, D), lambda t, j: (t, 0)),
            pl.BlockSpec(gffn.shape, lambda t, j: (0, 0)),
            pl.BlockSpec(wqT.shape, lambda t, j: (0, 0)),
            pl.BlockSpec(keys.shape, lambda t, j: (0, 0, 0)),
            pl.BlockSpec((E, D), lambda t, j: (j, 0)),
            pl.BlockSpec((D, E), lambda t, j: (0, j)),
        ],
        out_specs=pl.BlockSpec((T, D), lambda t, j: (t, 0)),
        out_shape=jax.ShapeDtypeStruct((N, D), F32),
        scratch_shapes=[
            pltpu.VMEM((D, T), BF16),
            pltpu.VMEM((D, T), F32),
            pltpu.VMEM((heads, n_keys, T), F32),
            pltpu.VMEM((heads, n_keys, T), F32),
            pltpu.VMEM((heads, n_keys, T), F32),
            pltpu.VMEM((heads, n_keys, T), F32),
            pltpu.VMEM((SUBLANES, T), F32),
            pltpu.VMEM((SUBLANES, T), F32),
            pltpu.VMEM((SUBLANES, T), F32),
            pltpu.VMEM((SUBLANES, T), F32),
            pltpu.VMEM((E, T), BF16),
        ],
        compiler_params=pltpu.CompilerParams(
            dimension_semantics=("arbitrary", "arbitrary"), vmem_limit_bytes=VMEM_LIMIT),
        name="peer",
    )(h, gffn, wqT, keys, u_bf, vT_bf)


def _ple_kernel(h_ref, p_ref, gple_ref, wgate_ref, wproj_ref, gfin_ref, out_ref, *, final):
    h = h_ref[...]
    hn = _rmsnorm(h, gple_ref[...]).astype(BF16)
    gate = _sigmoid(_dot(hn, wgate_ref[...]))
    proj = _dot(p_ref[...].astype(BF16), wproj_ref[...])
    h = h + proj * gate
    if final:
        h = _rmsnorm(h, gfin_ref[...])
    out_ref[...] = h


def _ple(h, p, gple, wgate, wproj, gfin, *, T, final):
    N, D = h.shape
    dp = p.shape[1]
    const2 = lambda t: (0, 0)
    return pl.pallas_call(
        functools.partial(_ple_kernel, final=final),
        grid=(N // T,),
        in_specs=[
            pl.BlockSpec((T, D), lambda t: (t, 0)),
            pl.BlockSpec((T, dp), lambda t: (t, 0)),
            pl.BlockSpec(gple.shape, const2),
            pl.BlockSpec(wgate.shape, const2),
            pl.BlockSpec(wproj.shape, const2),
            pl.BlockSpec(gfin.shape, const2),
        ],
        out_specs=pl.BlockSpec((T, D), lambda t: (t, 0)),
        out_shape=jax.ShapeDtypeStruct((N, D), F32),
        compiler_params=pltpu.CompilerParams(
            dimension_semantics=("arbitrary",), vmem_limit_bytes=VMEM_LIMIT),
        name="ple",
    )(h, p, gple, wgate, wproj, gfin)


def _block_diag(ws):
    g, n, _ = ws.shape
    out = jnp.zeros((g * n, g * n), ws.dtype)
    for k in range(g):
        out = out.at[k * n:(k + 1) * n, k * n:(k + 1) * n].set(ws[k])
    return out


def kernel(x_prompt, x_sample, state_pool, state_conv, p_prompt, p_sample, g_mix, w_in, w_pool, pool_scale, w_dw, b_dw, ln_g, ln_b, w_out, g_ffn, w_q, sub_keys, u_tab, v_tab, g_ple, w_ple_gate, w_ple_proj, g_final):
    depth = g_mix.shape[0]
    B, S, D = x_prompt.shape
    Bs, Ts, _ = x_sample.shape
    row = lambda v: v.reshape(1, -1)

    T_MIX = min(512, S)
    T_PEER = 512
    E_BLK = 1024
    TC = 256
    BC = min(64, Bs)

    hp = x_prompt
    hs = jnp.transpose(x_sample, (1, 0, 2))
    pool_p, conv_p, pool_s, conv_s = [], [], [], []
    for i in range(depth):
        win = w_in[i].astype(BF16)
        wpool_bd = _block_diag(w_pool[i]).astype(BF16)
        wout = w_out[i].astype(BF16)
        mix_w = (row(g_mix[i]), win, wpool_bd, row(pool_scale[i]), w_dw[i], row(b_dw[i]), row(ln_g[i]),
                 row(ln_b[i]), wout)
        wqT = w_q[i].T.astype(BF16)
        heads, _, n_keys, d_half = sub_keys[i].shape
        keys = sub_keys[i].reshape(heads * 2, n_keys, d_half).astype(BF16)
        u_bf = u_tab[i].astype(BF16)
        vT_bf = v_tab[i].T.astype(BF16)
        wgate = w_ple_gate[i].astype(BF16)
        wproj = w_ple_proj[i].astype(BF16)
        final = i == depth - 1

        hp, npool, nconv = _mixer_prompt(hp, *mix_w, T=T_MIX)
        pool_p.append(npool[0])
        conv_p.append(nconv[0])
        hs, spool, sconv = _mixer_sample(
            hs, jnp.transpose(state_pool[i], (1, 0, 2)), jnp.transpose(state_conv[i], (1, 0, 2)), *mix_w, BC=BC)
        pool_s.append(jnp.transpose(spool, (1, 0, 2)))
        conv_s.append(jnp.transpose(sconv, (1, 0, 2)))

        hp2 = _peer(hp.reshape(B * S, D), row(g_ffn[i]), wqT, keys, u_bf, vT_bf, T=T_PEER, E=E_BLK, TC=TC)
        hs2 = _peer(hs.reshape(Ts * Bs, D), row(g_ffn[i]), wqT, keys, u_bf, vT_bf,
                    T=min(T_PEER, Ts * Bs), E=E_BLK, TC=TC)
        pp = p_prompt[i].reshape(B * S, -1)
        ps = jnp.transpose(p_sample[i], (1, 0, 2)).reshape(Ts * Bs, -1)
        hp = _ple(hp2, pp, row(g_ple[i]), wgate, wproj, row(g_final), T=T_PEER, final=final).reshape(B, S, D)
        hs = _ple(hs2, ps, row(g_ple[i]), wgate, wproj, row(g_final), T=min(T_PEER, Ts * Bs),
                  final=final).reshape(Ts, Bs, D)

    y_prompt = hp
    y_sample = jnp.transpose(hs, (1, 0, 2))
    return (y_prompt, y_sample, jnp.stack(pool_p), jnp.stack(conv_p), jnp.stack(pool_s), jnp.stack(conv_s))
```

```python
import functools

import jax
import jax.numpy as jnp
from jax import lax
from jax.experimental import pallas as pl
from jax.experimental.pallas import tpu as pltpu

EPS = 1e-6
PAST_LEN = 16384
POOL_WINDOWS = (2, 4, 8, 16)
TOPK = 16

LANES = 128
SUBLANES = 8
PACK = 16
VMEM_LIMIT = 56 * 1024 * 1024

BF16 = jnp.bfloat16
F32 = jnp.float32


def _dot(a, b):
    return jnp.dot(a, b, preferred_element_type=F32)


def _rmsnorm(x, g):
    return x * lax.rsqrt(jnp.mean(x * x, axis=-1, keepdims=True) + EPS) * g


def _sigmoid(x):
    return 1.0 / (1.0 + jnp.exp(-x))


def _gelu_tanh(x):
    return 0.5 * x * (1.0 + jnp.tanh(0.7978845608028654 * (x + 0.044715 * (x * x * x))))


def _conv_tail(y, bdw, lng, lnb):
    y = y + bdw
    mu = jnp.mean(y, axis=-1, keepdims=True)
    d = y - mu
    var = jnp.mean(d * d, axis=-1, keepdims=True)
    yn = d * lax.rsqrt(var + EPS) * lng + lnb
    return yn * _sigmoid(yn)


A_PAD = 16
U_PAD = 32


def _mixer_prompt_kernel(x_ref, gmix_ref, win_ref, wpool_ref, pscale_ref, wdw_ref, bdw_ref, lng_ref, lnb_ref,
                         wout_ref, h_ref, pool_out_ref, conv_out_ref, abuf, ubuf, *, T, d_pool, d_conv):
    s = pl.program_id(1)
    n_s = pl.num_programs(1)
    conv_width = wdw_ref.shape[0]
    pool_state = max(POOL_WINDOWS) - 1
    conv_state = conv_width - 1

    @pl.when(s == 0)
    def _():
        abuf[0:A_PAD, :] = jnp.zeros((A_PAD, d_pool), F32)
        ubuf[0:U_PAD, :] = jnp.zeros((U_PAD, d_conv), F32)

    x = x_ref[0]
    xn = _rmsnorm(x, gmix_ref[...]).astype(BF16)
    z = _dot(xn, win_ref[...])
    a = z[:, :d_pool]
    val = z[:, d_pool:d_pool + d_conv]
    gt = z[:, d_pool + d_conv:]
    u = val * _sigmoid(gt)
    abuf[A_PAD:A_PAD + T, :] = a
    ubuf[U_PAD:U_PAD + T, :] = u

    group = d_pool // len(POOL_WINDOWS)
    pos = s * T + lax.broadcasted_iota(jnp.int32, (T, group), 0)
    ys = []
    for g, w in enumerate(POOL_WINDOWS):
        lo = g * group
        win = abuf[A_PAD:A_PAD + T, lo:lo + group]
        for jj in range(1, w):
            win = win + abuf[A_PAD - jj:A_PAD - jj + T, lo:lo + group]
        cnt = jnp.minimum(pos + 1, w).astype(F32)
        ys.append(win / cnt - a[:, lo:lo + group])
    ypool = jnp.concatenate(ys, axis=-1).astype(BF16)
    ya = _dot(ypool, wpool_ref[...]) * pscale_ref[...]

    off = U_PAD - conv_state
    y = ubuf[off:off + T, :] * wdw_ref[0:1, :]
    for k in range(1, conv_width):
        y = y + ubuf[off + k:off + k + T, :] * wdw_ref[k:k + 1, :]
    yb = _conv_tail(y, bdw_ref[...], lng_ref[...], lnb_ref[...])

    cat = jnp.concatenate([ya, yb], axis=-1).astype(BF16)
    h_ref[0] = x + _dot(cat, wout_ref[...])

    @pl.when(s == n_s - 1)
    def _():
        pool_out_ref[0, 0] = abuf[A_PAD + T - pool_state:A_PAD + T, :]
        conv_out_ref[0, 0] = ubuf[U_PAD + T - conv_state:U_PAD + T, :]

    abuf[0:A_PAD, :] = abuf[T:T + A_PAD, :]
    ubuf[0:U_PAD, :] = ubuf[T:T + U_PAD, :]


def _mixer_prompt(x, gmix, win, wpool_bd, pscale, wdw, bdw, lng, lnb, wout, *, T):
    B, S, D = x.shape
    d_pool = wpool_bd.shape[0]
    d_conv = wdw.shape[1]
    pool_state = max(POOL_WINDOWS) - 1
    conv_state = wdw.shape[0] - 1
    const2 = lambda b, s: (0, 0)
    kern = functools.partial(_mixer_prompt_kernel, T=T, d_pool=d_pool, d_conv=d_conv)
    return pl.pallas_call(
        kern,
        grid=(B, S // T),
        in_specs=[
            pl.BlockSpec((1, T, D), lambda b, s: (b, s, 0)),
            pl.BlockSpec(gmix.shape, const2),
            pl.BlockSpec(win.shape, const2),
            pl.BlockSpec(wpool_bd.shape, const2),
            pl.BlockSpec(pscale.shape, const2),
            pl.BlockSpec(wdw.shape, const2),
            pl.BlockSpec(bdw.shape, const2),
            pl.BlockSpec(lng.shape, const2),
            pl.BlockSpec(lnb.shape, const2),
            pl.BlockSpec(wout.shape, const2),
        ],
        out_specs=[
            pl.BlockSpec((1, T, D), lambda b, s: (b, s, 0)),
            pl.BlockSpec((1, 1, pool_state, d_pool), lambda b, s: (0, b, 0, 0)),
            pl.BlockSpec((1, 1, conv_state, d_conv), lambda b, s: (0, b, 0, 0)),
        ],
        out_shape=[
            jax.ShapeDtypeStruct((B, S, D), F32),
            jax.ShapeDtypeStruct((1, B, pool_state, d_pool), F32),
            jax.ShapeDtypeStruct((1, B, conv_state, d_conv), F32),
        ],
        scratch_shapes=[
            pltpu.VMEM((T + A_PAD, d_pool), F32),
            pltpu.VMEM((T + U_PAD, d_conv), F32),
        ],
        compiler_params=pltpu.CompilerParams(
            dimension_semantics=("arbitrary", "arbitrary"), vmem_limit_bytes=VMEM_LIMIT),
        name="mixer_prompt",
    )(x, gmix, win, wpool_bd, pscale, wdw, bdw, lng, lnb, wout)


def _mixer_sample_kernel(x_ref, pool_ref, conv_ref, gmix_ref, win_ref, wpool_ref, pscale_ref, wdw_ref, bdw_ref,
                         lng_ref, lnb_ref, wout_ref, h_ref, pool_out_ref, conv_out_ref, *, d_pool, d_conv):
    Ts, BC, D = x_ref.shape
    P = pool_ref.shape[0]
    C = conv_ref.shape[0]
    conv_width = wdw_ref.shape[0]

    x = x_ref[...].reshape(Ts * BC, D)
    xn = _rmsnorm(x, gmix_ref[...]).astype(BF16)
    z = _dot(xn, win_ref[...])
    a = z[:, :d_pool]
    u = z[:, d_pool:d_pool + d_conv] * _sigmoid(z[:, d_pool + d_conv:])

    def full_a(k):
        return pool_ref[k] if k < P else a[(k - P) * BC:(k - P + 1) * BC, :]

    def full_u(k):
        return conv_ref[k] if k < C else u[(k - C) * BC:(k - C + 1) * BC, :]

    group = d_pool // len(POOL_WINDOWS)
    ya_rows, yb_rows = [], []
    for t in range(Ts):
        ys = []
        for g, w in enumerate(POOL_WINDOWS):
            lo = g * group
            win = full_a(P + t)[:, lo:lo + group]
            for jj in range(1, w):
                win = win + full_a(P + t - jj)[:, lo:lo + group]
            cnt = float(min(PAST_LEN + t + 1, w))
            ys.append(win / cnt - full_a(P + t)[:, lo:lo + group])
        ya_rows.append(jnp.concatenate(ys, axis=-1))
        y = full_u(t) * wdw_ref[0:1, :]
        for k in range(1, conv_width):
            y = y + full_u(t + k) * wdw_ref[k:k + 1, :]
        yb_rows.append(y)
    ypool = jnp.concatenate(ya_rows, axis=0).astype(BF16)
    ya = _dot(ypool, wpool_ref[...]) * pscale_ref[...]
    yb = _conv_tail(jnp.concatenate(yb_rows, axis=0), bdw_ref[...], lng_ref[...], lnb_ref[...])
    cat = jnp.concatenate([ya, yb], axis=-1).astype(BF16)
    h = x + _dot(cat, wout_ref[...])
    h_ref[...] = h.reshape(Ts, BC, D)
    for k in range(P):
        pool_out_ref[k] = full_a(Ts + k)
    for k in range(C):
        conv_out_ref[k] = full_u(Ts + k)


def _mixer_sample(x_tm, pool_tm, conv_tm, gmix, win, wpool_bd, pscale, wdw, bdw, lng, lnb, wout, *, BC):
    Ts, B, D = x_tm.shape
    P, _, d_pool = pool_tm.shape
    C, _, d_conv = conv_tm.shape
    const2 = lambda b: (0, 0)
    kern = functools.partial(_mixer_sample_kernel, d_pool=d_pool, d_conv=d_conv)
    return pl.pallas_call(
        kern,
        grid=(B // BC,),
        in_specs=[
            pl.BlockSpec((Ts, BC, D), lambda b: (0, b, 0)),
            pl.BlockSpec((P, BC, d_pool), lambda b: (0, b, 0)),
            pl.BlockSpec((C, BC, d_conv), lambda b: (0, b, 0)),
            pl.BlockSpec(gmix.shape, const2),
            pl.BlockSpec(win.shape, const2),
            pl.BlockSpec(wpool_bd.shape, const2),
            pl.BlockSpec(pscale.shape, const2),
            pl.BlockSpec(wdw.shape, const2),
            pl.BlockSpec(bdw.shape, const2),
            pl.BlockSpec(lng.shape, const2),
            pl.BlockSpec(lnb.shape, const2),
            pl.BlockSpec(wout.shape, const2),
        ],
        out_specs=[
            pl.BlockSpec((Ts, BC, D), lambda b: (0, b, 0)),
            pl.BlockSpec((P, BC, d_pool), lambda b: (0, b, 0)),
            pl.BlockSpec((C, BC, d_conv), lambda b: (0, b, 0)),
        ],
        out_shape=[
            jax.ShapeDtypeStruct((Ts, B, D), F32),
            jax.ShapeDtypeStruct((P, B, d_pool), F32),
            jax.ShapeDtypeStruct((C, B, d_conv), F32),
        ],
        compiler_params=pltpu.CompilerParams(
            dimension_semantics=("arbitrary",), vmem_limit_bytes=VMEM_LIMIT),
        name="mixer_sample",
    )(x_tm, pool_tm, conv_tm, gmix, win, wpool_bd, pscale, wdw, bdw, lng, lnb, wout)


def _batcher_pairs(n):
    pairs = []
    t = (n - 1).bit_length()
    p = 1 << (t - 1)
    while p > 0:
        q = 1 << (t - 1)
        r = 0
        d = p
        while d > 0:
            for i in range(n - d):
                if (i & p) == r:
                    pairs.append((i, i + d))
            d = q - p
            q >>= 1
            r = p
        p >>= 1
    return pairs


_SORT16 = _batcher_pairs(TOPK)


def _top16_sorted(s):
    n = s.shape[0] // SUBLANES
    v = [s[k * SUBLANES:(k + 1) * SUBLANES, :] for k in range(n)]
    for i, j in _SORT16:
        hi = jnp.maximum(v[i], v[j])
        lo = jnp.minimum(v[i], v[j])
        v[i], v[j] = hi, lo
    for sh in (4, 2, 1):
        b = [pltpu.roll(x, sh, 0) for x in v]
        v = [jnp.maximum(v[k], b[n - 1 - k]) for k in range(n)]
        st = n // 2
        while st >= 1:
            for k in range(n):
                if not (k & st):
                    hi = jnp.maximum(v[k], v[k + st])
                    lo = jnp.minimum(v[k], v[k + st])
                    v[k], v[k + st] = hi, lo
            st //= 2
    return v


def _kth_largest(cands, k):
    def tree_max(xs):
        xs = list(xs)
        while len(xs) > 1:
            nxt = [jnp.maximum(xs[i], xs[i + 1]) for i in range(0, len(xs) - 1, 2)]
            if len(xs) % 2:
                nxt.append(xs[-1])
            xs = nxt
        return xs[0]

    cur = list(cands)
    for _ in range(k - 1):
        m = tree_max(cur)
        cur = [jnp.where(c == m, -jnp.inf, c) for c in cur]
    return tree_max(cur)


def _peer_kernel(h_ref, gffn_ref, wqT_ref, keys_ref, u_ref, vT_ref, out_ref,
                 xnT_ref, accT_ref, s0_ref, s1_ref, r1_ref, e1_ref, lfp_ref, e0p_ref,
                 sv0_ref, sv1_ref, lcnt_ref, m0_ref, m1_ref, zinv_ref, coef_ref,
                 *, T, E, TC, heads):
    j = pl.program_id(1)
    n_j = pl.num_programs(1)
    n_keys = keys_ref.shape[1]
    d_half = keys_ref.shape[2]
    n_tiles = T // LANES
    i_per_step = E // n_keys
    assert i_per_step == SUBLANES and n_keys == 8 * PACK

    @pl.when(j == 0)
    def _prep():
        xn = _rmsnorm(h_ref[...], gffn_ref[...])
        xnT = xn.T.astype(BF16)
        xnT_ref[...] = xnT
        accT_ref[...] = jnp.zeros(accT_ref.shape, F32)
        for hd in range(heads):
            qT = _dot(wqT_ref[hd * 2 * d_half:(hd + 1) * 2 * d_half, :], xnT).astype(BF16)
            s0_ref[hd] = _dot(keys_ref[2 * hd], qT[:d_half, :])
            s1_ref[hd] = _dot(keys_ref[2 * hd + 1], qT[d_half:, :])

        sub = lax.broadcasted_iota(jnp.int32, (SUBLANES, LANES), 0)
        pq = [(p, q) for p in range(TOPK) for q in range(TOPK) if (p + 1) * (q + 1) <= TOPK]

        def tile_body(c, carry):
            sl = pl.ds(pl.multiple_of(c * LANES, LANES), LANES)
            sv0 = [jnp.zeros((SUBLANES, LANES), F32)] * TOPK
            sv1 = [jnp.zeros((SUBLANES, LANES), F32)] * TOPK
            for hd in range(heads):
                t0 = _top16_sorted(s0_ref[hd, :, sl])
                t1 = _top16_sorted(s1_ref[hd, :, sl])
                sv0 = [jnp.where(sub == hd, t0[k], sv0[k]) for k in range(TOPK)]
                sv1 = [jnp.where(sub == hd, t1[k], sv1[k]) for k in range(TOPK)]
            cands = [sv0[p] + sv1[q] for p, q in pq]
            thr = _kth_largest(cands, TOPK)
            m0 = sv0[0]
            m1 = sv1[0]
            ex0 = [jnp.exp(x - m0) for x in sv0]
            ex1 = [jnp.exp(x - m1) for x in sv1]
            z = jnp.zeros((SUBLANES, LANES), F32)
            cnt = [jnp.zeros((SUBLANES, LANES), F32)] * TOPK
            for (p, q), cnd in zip(pq, cands):
                sel = cnd >= thr
                z = z + jnp.where(sel, ex0[p] * ex1[q], 0.0)
                cnt[p] = cnt[p] + jnp.where(sel, 1.0, 0.0)
            for k in range(TOPK):
                sv0_ref[k, :, sl] = sv0[k]
                sv1_ref[k, :, sl] = sv1[k]
                lcnt_ref[k, :, sl] = cnt[k]
            m0_ref[:, sl] = m0
            m1_ref[:, sl] = m1
            zinv_ref[:, sl] = 1.0 / z

            for hd in range(heads):
                hrow = slice(hd, hd + 1)
                s0t = s0_ref[hd, :, sl]
                s1t = s1_ref[hd, :, sl]
                lf = jnp.broadcast_to(lcnt_ref[0, hrow, sl], s0t.shape)
                r1 = jnp.zeros(s1t.shape, F32)
                for k in range(TOPK):
                    nxt = lcnt_ref[k + 1, hrow, sl] if k + 1 < TOPK else 0.0
                    lf = jnp.where(sv0_ref[k, hrow, sl] > s0t, nxt, lf)
                    r1 = jnp.where(sv1_ref[k, hrow, sl] > s1t, float(k + 1), r1)
                e0 = jnp.exp(s0t - m0_ref[hrow, sl]) * zinv_ref[hrow, sl]
                e1 = jnp.exp(s1t - m1_ref[hrow, sl])
                r1b = r1.astype(BF16)
                e1b = e1.astype(BF16)
                for g in range(n_keys // PACK):
                    r1_ref[hd, g, :, sl] = r1b[g * PACK:(g + 1) * PACK, :]
                    e1_ref[hd, g, :, sl] = e1b[g * PACK:(g + 1) * PACK, :]
                lfp_ref[hd, c] = lf.reshape(n_keys // SUBLANES, SUBLANES, LANES)
                e0p_ref[hd, c] = e0.reshape(n_keys // SUBLANES, SUBLANES, LANES)
            return carry

        lax.fori_loop(0, n_tiles, tile_body, 0)

    def bcast_row(ref, hd, c, i):
        parts = [jnp.broadcast_to(ref[hd, c * (TC // LANES) + k, j][i:i + 1, :], (PACK, LANES)).astype(BF16)
                 for k in range(TC // LANES)]
        return jnp.concatenate(parts, axis=-1)[None]

    zero_b = jnp.zeros((), BF16)
    for c in range(T // TC):
        cs = slice(c * TC, (c + 1) * TC)
        st = _dot(u_ref[...], xnT_ref[:, cs])
        for i in range(i_per_step):
            gsum = jnp.zeros((n_keys // PACK, PACK, TC), BF16)
            for hd in range(heads):
                mask = r1_ref[hd, :, :, cs] < bcast_row(lfp_ref, hd, c, i)
                gsum = gsum + jnp.where(mask, e1_ref[hd, :, :, cs], zero_b) * bcast_row(e0p_ref, hd, c, i)
            hval = _gelu_tanh(st[i * n_keys:(i + 1) * n_keys, :]).astype(BF16)
            for g in range(n_keys // PACK):
                lo = i * n_keys + g * PACK
                coef_ref[lo:lo + PACK, cs] = hval[g * PACK:(g + 1) * PACK, :] * gsum[g]
        accT_ref[:, cs] += _dot(vT_ref[...], coef_ref[:, cs])

    @pl.when(j == n_j - 1)
    def _fin():
        out_ref[...] = h_ref[...] + accT_ref[...].T


def _peer(h, gffn, wqT, keys, u_bf, vT_bf, *, T, E, TC):
    N, D = h.shape
    heads = keys.shape[0] // 2
    n_keys = keys.shape[1]
    n_exp = u_bf.shape[0]
    kern = functools.partial(_peer_kernel, T=T, E=E, TC=TC, heads=heads)
    return pl.pallas_call(
        kern,
        grid=(N // T, n_exp // E),
        in_specs=[
            pl.BlockSpec((T, D), lambda t, j: (t, 0)),
            pl.BlockSpec(gffn.shape, lambda t, j: (0, 0)),
            pl.BlockSpec(wqT.shape, lambda t, j: (0, 0)),
            pl.BlockSpec(keys.shape, lambda t, j: (0, 0, 0)),
            pl.BlockSpec((E, D), lambda t, j: (j, 0)),
            pl.BlockSpec((D, E), lambda t, j: (0, j)),
        ],
        out_specs=pl.BlockSpec((T, D), lambda t, j: (t, 0)),
        out_shape=jax.ShapeDtypeStruct((N, D), F32),
        scratch_shapes=[
            pltpu.VMEM((D, T), BF16),
            pltpu.VMEM((D, T), F32),
            pltpu.VMEM((heads, n_keys, T), F32),
            pltpu.VMEM((heads, n_keys, T), F32),
            pltpu.VMEM((heads, n_keys // PACK, PACK, T), BF16),
            pltpu.VMEM((heads, n_keys // PACK, PACK, T), BF16),
            pltpu.VMEM((heads, T // LANES, n_keys // SUBLANES, SUBLANES, LANES), F32),
            pltpu.VMEM((heads, T // LANES, n_keys // SUBLANES, SUBLANES, LANES), F32),
            pltpu.VMEM((TOPK, SUBLANES, T), F32),
            pltpu.VMEM((TOPK, SUBLANES, T), F32),
            pltpu.VMEM((TOPK, SUBLANES, T), F32),
            pltpu.VMEM((SUBLANES, T), F32),
            pltpu.VMEM((SUBLANES, T), F32),
            pltpu.VMEM((SUBLANES, T), F32),
            pltpu.VMEM((E, T), BF16),
        ],
        compiler_params=pltpu.CompilerParams(
            dimension_semantics=("arbitrary", "arbitrary"), vmem_limit_bytes=VMEM_LIMIT),
        name="peer",
    )(h, gffn, wqT, keys, u_bf, vT_bf)


def _ple_kernel(h_ref, p_ref, gple_ref, wgate_ref, wproj_ref, gfin_ref, out_ref, *, final):
    h = h_ref[...]
    hn = _rmsnorm(h, gple_ref[...]).astype(BF16)
    gate = _sigmoid(_dot(hn, wgate_ref[...]))
    proj = _dot(p_ref[...].astype(BF16), wproj_ref[...])
    h = h + proj * gate
    if final:
        h = _rmsnorm(h, gfin_ref[...])
    out_ref[...] = h


def _ple(h, p, gple, wgate, wproj, gfin, *, T, final):
    N, D = h.shape
    dp = p.shape[1]
    const2 = lambda t: (0, 0)
    return pl.pallas_call(
        functools.partial(_ple_kernel, final=final),
        grid=(N // T,),
        in_specs=[
            pl.BlockSpec((T, D), lambda t: (t, 0)),
            pl.BlockSpec((T, dp), lambda t: (t, 0)),
            pl.BlockSpec(gple.shape, const2),
            pl.BlockSpec(wgate.shape, const2),
            pl.BlockSpec(wproj.shape, const2),
            pl.BlockSpec(gfin.shape, const2),
        ],
        out_specs=pl.BlockSpec((T, D), lambda t: (t, 0)),
        out_shape=jax.ShapeDtypeStruct((N, D), F32),
        compiler_params=pltpu.CompilerParams(
            dimension_semantics=("arbitrary",), vmem_limit_bytes=VMEM_LIMIT),
        name="ple",
    )(h, p, gple, wgate, wproj, gfin)


def _block_diag(ws):
    g, n, _ = ws.shape
    out = jnp.zeros((g * n, g * n), ws.dtype)
    for k in range(g):
        out = out.at[k * n:(k + 1) * n, k * n:(k + 1) * n].set(ws[k])
    return out


def kernel(x_prompt, x_sample, state_pool, state_conv, p_prompt, p_sample, g_mix, w_in, w_pool, pool_scale, w_dw, b_dw, ln_g, ln_b, w_out, g_ffn, w_q, sub_keys, u_tab, v_tab, g_ple, w_ple_gate, w_ple_proj, g_final):
    depth = g_mix.shape[0]
    B, S, D = x_prompt.shape
    Bs, Ts, _ = x_sample.shape
    row = lambda v: v.reshape(1, -1)

    T_MIX = min(512, S)
    T_PEER = 512
    E_BLK = 1024
    TC = 256
    BC = min(64, Bs)

    hp = x_prompt
    hs = jnp.transpose(x_sample, (1, 0, 2))
    pool_p, conv_p, pool_s, conv_s = [], [], [], []
    for i in range(depth):
        win = w_in[i].astype(BF16)
        wpool_bd = _block_diag(w_pool[i]).astype(BF16)
        wout = w_out[i].astype(BF16)
        mix_w = (row(g_mix[i]), win, wpool_bd, row(pool_scale[i]), w_dw[i], row(b_dw[i]), row(ln_g[i]),
                 row(ln_b[i]), wout)
        wqT = w_q[i].T.astype(BF16)
        heads, _, n_keys, d_half = sub_keys[i].shape
        keys = sub_keys[i].reshape(heads * 2, n_keys, d_half).astype(BF16)
        u_bf = u_tab[i].astype(BF16)
        vT_bf = v_tab[i].T.astype(BF16)
        wgate = w_ple_gate[i].astype(BF16)
        wproj = w_ple_proj[i].astype(BF16)
        final = i == depth - 1

        hp, npool, nconv = _mixer_prompt(hp, *mix_w, T=T_MIX)
        pool_p.append(npool[0])
        conv_p.append(nconv[0])
        hs, spool, sconv = _mixer_sample(
            hs, jnp.transpose(state_pool[i], (1, 0, 2)), jnp.transpose(state_conv[i], (1, 0, 2)), *mix_w, BC=BC)
        pool_s.append(jnp.transpose(spool, (1, 0, 2)))
        conv_s.append(jnp.transpose(sconv, (1, 0, 2)))

        hp2 = _peer(hp.reshape(B * S, D), row(g_ffn[i]), wqT, keys, u_bf, vT_bf, T=T_PEER, E=E_BLK, TC=TC)
        hs2 = _peer(hs.reshape(Ts * Bs, D), row(g_ffn[i]), wqT, keys, u_bf, vT_bf,
                    T=min(T_PEER, Ts * Bs), E=E_BLK, TC=TC)
        pp = p_prompt[i].reshape(B * S, -1)
        ps = jnp.transpose(p_sample[i], (1, 0, 2)).reshape(Ts * Bs, -1)
        hp = _ple(hp2, pp, row(g_ple[i]), wgate, wproj, row(g_final), T=T_PEER, final=final).reshape(B, S, D)
        hs = _ple(hs2, ps, row(g_ple[i]), wgate, wproj, row(g_final), T=min(T_PEER, Ts * Bs),
                  final=final).reshape(Ts, Bs, D)

    y_prompt = hp
    y_sample = jnp.transpose(hs, (1, 0, 2))
    return (y_prompt, y_sample, jnp.stack(pool_p), jnp.stack(conv_p), jnp.stack(pool_s), jnp.stack(conv_s))
```

```python
import functools

import jax
import jax.numpy as jnp
from jax import lax
from jax.experimental import pallas as pl
from jax.experimental.pallas import tpu as pltpu

EPS = 1e-6
PAST_LEN = 16384
POOL_WINDOWS = (2, 4, 8, 16)
TOPK = 16

LANES = 128
SUBLANES = 8
PACK = 16
VMEM_LIMIT = 56 * 1024 * 1024

BF16 = jnp.bfloat16
F32 = jnp.float32


def _dot(a, b):
    return jnp.dot(a, b, preferred_element_type=F32)


def _rmsnorm(x, g):
    return x * lax.rsqrt(jnp.mean(x * x, axis=-1, keepdims=True) + EPS) * g


def _sigmoid(x):
    return 1.0 / (1.0 + jnp.exp(-x))


def _gelu_tanh(x):
    return 0.5 * x * (1.0 + jnp.tanh(0.7978845608028654 * (x + 0.044715 * (x * x * x))))


def _conv_tail(y, bdw, lng, lnb):
    y = y + bdw
    mu = jnp.mean(y, axis=-1, keepdims=True)
    d = y - mu
    var = jnp.mean(d * d, axis=-1, keepdims=True)
    yn = d * lax.rsqrt(var + EPS) * lng + lnb
    return yn * _sigmoid(yn)


A_PAD = 16
U_PAD = 32


def _mixer_prompt_kernel(x_ref, gmix_ref, win_ref, wpool_ref, pscale_ref, wdw_ref, bdw_ref, lng_ref, lnb_ref,
                         wout_ref, h_ref, pool_out_ref, conv_out_ref, abuf, ubuf, *, T, d_pool, d_conv):
    s = pl.program_id(1)
    n_s = pl.num_programs(1)
    conv_width = wdw_ref.shape[0]
    pool_state = max(POOL_WINDOWS) - 1
    conv_state = conv_width - 1

    @pl.when(s == 0)
    def _():
        abuf[0:A_PAD, :] = jnp.zeros((A_PAD, d_pool), F32)
        ubuf[0:U_PAD, :] = jnp.zeros((U_PAD, d_conv), F32)

    x = x_ref[0]
    xn = _rmsnorm(x, gmix_ref[...]).astype(BF16)
    z = _dot(xn, win_ref[...])
    a = z[:, :d_pool]
    val = z[:, d_pool:d_pool + d_conv]
    gt = z[:, d_pool + d_conv:]
    u = val * _sigmoid(gt)
    abuf[A_PAD:A_PAD + T, :] = a
    ubuf[U_PAD:U_PAD + T, :] = u

    group = d_pool // len(POOL_WINDOWS)
    pos = s * T + lax.broadcasted_iota(jnp.int32, (T, group), 0)
    ys = []
    for g, w in enumerate(POOL_WINDOWS):
        lo = g * group
        win = abuf[A_PAD:A_PAD + T, lo:lo + group]
        for jj in range(1, w):
            win = win + abuf[A_PAD - jj:A_PAD - jj + T, lo:lo + group]
        cnt = jnp.minimum(pos + 1, w).astype(F32)
        ys.append(win / cnt - a[:, lo:lo + group])
    ypool = jnp.concatenate(ys, axis=-1).astype(BF16)
    ya = _dot(ypool, wpool_ref[...]) * pscale_ref[...]

    off = U_PAD - conv_state
    y = ubuf[off:off + T, :] * wdw_ref[0:1, :]
    for k in range(1, conv_width):
        y = y + ubuf[off + k:off + k + T, :] * wdw_ref[k:k + 1, :]
    yb = _conv_tail(y, bdw_ref[...], lng_ref[...], lnb_ref[...])

    cat = jnp.concatenate([ya, yb], axis=-1).astype(BF16)
    h_ref[0] = x + _dot(cat, wout_ref[...])

    @pl.when(s == n_s - 1)
    def _():
        pool_out_ref[0, 0] = abuf[A_PAD + T - pool_state:A_PAD + T, :]
        conv_out_ref[0, 0] = ubuf[U_PAD + T - conv_state:U_PAD + T, :]

    abuf[0:A_PAD, :] = abuf[T:T + A_PAD, :]
    ubuf[0:U_PAD, :] = ubuf[T:T + U_PAD, :]


def _mixer_prompt(x, gmix, win, wpool_bd, pscale, wdw, bdw, lng, lnb, wout, *, T):
    B, S, D = x.shape
    d_pool = wpool_bd.shape[0]
    d_conv = wdw.shape[1]
    pool_state = max(POOL_WINDOWS) - 1
    conv_state = wdw.shape[0] - 1
    const2 = lambda b, s: (0, 0)
    kern = functools.partial(_mixer_prompt_kernel, T=T, d_pool=d_pool, d_conv=d_conv)
    return pl.pallas_call(
        kern,
        grid=(B, S // T),
        in_specs=[
            pl.BlockSpec((1, T, D), lambda b, s: (b, s, 0)),
            pl.BlockSpec(gmix.shape, const2),
            pl.BlockSpec(win.shape, const2),
            pl.BlockSpec(wpool_bd.shape, const2),
            pl.BlockSpec(pscale.shape, const2),
            pl.BlockSpec(wdw.shape, const2),
            pl.BlockSpec(bdw.shape, const2),
            pl.BlockSpec(lng.shape, const2),
            pl.BlockSpec(lnb.shape, const2),
            pl.BlockSpec(wout.shape, const2),
        ],
        out_specs=[
            pl.BlockSpec((1, T, D), lambda b, s: (b, s, 0)),
            pl.BlockSpec((1, 1, pool_state, d_pool), lambda b, s: (0, b, 0, 0)),
            pl.BlockSpec((1, 1, conv_state, d_conv), lambda b, s: (0, b, 0, 0)),
        ],
        out_shape=[
            jax.ShapeDtypeStruct((B, S, D), F32),
            jax.ShapeDtypeStruct((1, B, pool_state, d_pool), F32),
            jax.ShapeDtypeStruct((1, B, conv_state, d_conv), F32),
        ],
        scratch_shapes=[
            pltpu.VMEM((T + A_PAD, d_pool), F32),
            pltpu.VMEM((T + U_PAD, d_conv), F32),
        ],
        compiler_params=pltpu.CompilerParams(
            dimension_semantics=("arbitrary", "arbitrary"), vmem_limit_bytes=VMEM_LIMIT),
        name="mixer_prompt",
    )(x, gmix, win, wpool_bd, pscale, wdw, bdw, lng, lnb, wout)


def _mixer_sample_kernel(x_ref, pool_ref, conv_ref, gmix_ref, win_ref, wpool_ref, pscale_ref, wdw_ref, bdw_ref,
                         lng_ref, lnb_ref, wout_ref, h_ref, pool_out_ref, conv_out_ref, *, d_pool, d_conv):
    Ts, BC, D = x_ref.shape
    P = pool_ref.shape[0]
    C = conv_ref.shape[0]
    conv_width = wdw_ref.shape[0]

    x = x_ref[...].reshape(Ts * BC, D)
    xn = _rmsnorm(x, gmix_ref[...]).astype(BF16)
    z = _dot(xn, win_ref[...])
    a = z[:, :d_pool]
    u = z[:, d_pool:d_pool + d_conv] * _sigmoid(z[:, d_pool + d_conv:])

    def full_a(k):
        return pool_ref[k] if k < P else a[(k - P) * BC:(k - P + 1) * BC, :]

    def full_u(k):
        return conv_ref[k] if k < C else u[(k - C) * BC:(k - C + 1) * BC, :]

    group = d_pool // len(POOL_WINDOWS)
    ya_rows, yb_rows = [], []
    for t in range(Ts):
        ys = []
        for g, w in enumerate(POOL_WINDOWS):
            lo = g * group
            win = full_a(P + t)[:, lo:lo + group]
            for jj in range(1, w):
                win = win + full_a(P + t - jj)[:, lo:lo + group]
            cnt = float(min(PAST_LEN + t + 1, w))
            ys.append(win / cnt - full_a(P + t)[:, lo:lo + group])
        ya_rows.append(jnp.concatenate(ys, axis=-1))
        y = full_u(t) * wdw_ref[0:1, :]
        for k in range(1, conv_width):
            y = y + full_u(t + k) * wdw_ref[k:k + 1, :]
        yb_rows.append(y)
    ypool = jnp.concatenate(ya_rows, axis=0).astype(BF16)
    ya = _dot(ypool, wpool_ref[...]) * pscale_ref[...]
    yb = _conv_tail(jnp.concatenate(yb_rows, axis=0), bdw_ref[...], lng_ref[...], lnb_ref[...])
    cat = jnp.concatenate([ya, yb], axis=-1).astype(BF16)
    h = x + _dot(cat, wout_ref[...])
    h_ref[...] = h.reshape(Ts, BC, D)
    for k in range(P):
        pool_out_ref[k] = full_a(Ts + k)
    for k in range(C):
        conv_out_ref[k] = full_u(Ts + k)


def _mixer_sample(x_tm, pool_tm, conv_tm, gmix, win, wpool_bd, pscale, wdw, bdw, lng, lnb, wout, *, BC):
    Ts, B, D = x_tm.shape
    P, _, d_pool = pool_tm.shape
    C, _, d_conv = conv_tm.shape
    const2 = lambda b: (0, 0)
    kern = functools.partial(_mixer_sample_kernel, d_pool=d_pool, d_conv=d_conv)
    return pl.pallas_call(
        kern,
        grid=(B // BC,),
        in_specs=[
            pl.BlockSpec((Ts, BC, D), lambda b: (0, b, 0)),
            pl.BlockSpec((P, BC, d_pool), lambda b: (0, b, 0)),
            pl.BlockSpec((C, BC, d_conv), lambda b: (0, b, 0)),
            pl.BlockSpec(gmix.shape, const2),
            pl.BlockSpec(win.shape, const2),
            pl.BlockSpec(wpool_bd.shape, const2),
            pl.BlockSpec(pscale.shape, const2),
            pl.BlockSpec(wdw.shape, const2),
            pl.BlockSpec(bdw.shape, const2),
            pl.BlockSpec(lng.shape, const2),
            pl.BlockSpec(lnb.shape, const2),
            pl.BlockSpec(wout.shape, const2),
        ],
        out_specs=[
            pl.BlockSpec((Ts, BC, D), lambda b: (0, b, 0)),
            pl.BlockSpec((P, BC, d_pool), lambda b: (0, b, 0)),
            pl.BlockSpec((C, BC, d_conv), lambda b: (0, b, 0)),
        ],
        out_shape=[
            jax.ShapeDtypeStruct((Ts, B, D), F32),
            jax.ShapeDtypeStruct((P, B, d_pool), F32),
            jax.ShapeDtypeStruct((C, B, d_conv), F32),
        ],
        compiler_params=pltpu.CompilerParams(
            dimension_semantics=("arbitrary",), vmem_limit_bytes=VMEM_LIMIT),
        name="mixer_sample",
    )(x_tm, pool_tm, conv_tm, gmix, win, wpool_bd, pscale, wdw, bdw, lng, lnb, wout)


def _batcher_pairs(n):
    pairs = []
    t = (n - 1).bit_length()
    p = 1 << (t - 1)
    while p > 0:
        q = 1 << (t - 1)
        r = 0
        d = p
        while d > 0:
            for i in range(n - d):
                if (i & p) == r:
                    pairs.append((i, i + d))
            d = q - p
            q >>= 1
            r = p
        p >>= 1
    return pairs


_SORT16 = _batcher_pairs(TOPK)


def _top16_sorted(s):
    n = s.shape[0] // SUBLANES
    v = [s[k * SUBLANES:(k + 1) * SUBLANES, :] for k in range(n)]
    for i, j in _SORT16:
        hi = jnp.maximum(v[i], v[j])
        lo = jnp.minimum(v[i], v[j])
        v[i], v[j] = hi, lo
    for sh in (4, 2, 1):
        b = [pltpu.roll(x, sh, 0) for x in v]
        v = [jnp.maximum(v[k], b[n - 1 - k]) for k in range(n)]
        st = n // 2
        while st >= 1:
            for k in range(n):
                if not (k & st):
                    hi = jnp.maximum(v[k], v[k + st])
                    lo = jnp.minimum(v[k], v[k + st])
                    v[k], v[k + st] = hi, lo
            st //= 2
    return v


def _kth_largest(cands, k):
    def tree_max(xs):
        xs = list(xs)
        while len(xs) > 1:
            nxt = [jnp.maximum(xs[i], xs[i + 1]) for i in range(0, len(xs) - 1, 2)]
            if len(xs) % 2:
                nxt.append(xs[-1])
            xs = nxt
        return xs[0]

    cur = list(cands)
    for _ in range(k - 1):
        m = tree_max(cur)
        cur = [jnp.where(c == m, -jnp.inf, c) for c in cur]
    return tree_max(cur)


def _peer_kernel(h_ref, gffn_ref, wqT_ref, keys_ref, u_ref, vT_ref, out_ref,
                 xnT_ref, accT_ref, sa_ref, sb_ref, ca_ref, cb_ref, r1_ref, e1_ref, lfp_ref, e0p_ref,
                 sv0_ref, sv1_ref, lcnt_ref, m0_ref, m1_ref, zinv_ref,
                 *, T, E, W, heads):
    j = pl.program_id(1)
    n_pairs = pl.num_programs(1) - 1
    n_keys = keys_ref.shape[1]
    d_half = keys_ref.shape[2]
    n_tiles = T // LANES
    i_per_step = E // n_keys
    assert i_per_step == SUBLANES and n_keys == 8 * PACK and heads <= sa_ref.shape[0]
    s0_ref, s1_ref = sa_ref, sb_ref

    def prep():
        xn = _rmsnorm(h_ref[...], gffn_ref[...])
        xnT = xn.T.astype(BF16)
        xnT_ref[...] = xnT
        accT_ref[...] = jnp.zeros(accT_ref.shape, F32)
        for hd in range(heads):
            qT = _dot(wqT_ref[hd * 2 * d_half:(hd + 1) * 2 * d_half, :], xnT).astype(BF16)
            s0_ref[hd] = _dot(keys_ref[2 * hd], qT[:d_half, :])
            s1_ref[hd] = _dot(keys_ref[2 * hd + 1], qT[d_half:, :])

        sub = lax.broadcasted_iota(jnp.int32, (SUBLANES, LANES), 0)
        pq = [(p, q) for p in range(TOPK) for q in range(TOPK) if (p + 1) * (q + 1) <= TOPK]

        def tile_body(c, carry):
            sl = pl.ds(pl.multiple_of(c * LANES, LANES), LANES)
            sv0 = [jnp.zeros((SUBLANES, LANES), F32)] * TOPK
            sv1 = [jnp.zeros((SUBLANES, LANES), F32)] * TOPK
            for hd in range(heads):
                t0 = _top16_sorted(s0_ref[hd, :, sl])
                t1 = _top16_sorted(s1_ref[hd, :, sl])
                sv0 = [jnp.where(sub == hd, t0[k], sv0[k]) for k in range(TOPK)]
                sv1 = [jnp.where(sub == hd, t1[k], sv1[k]) for k in range(TOPK)]
            cands = [sv0[p] + sv1[q] for p, q in pq]
            thr = _kth_largest(cands, TOPK)
            m0 = sv0[0]
            m1 = sv1[0]
            ex0 = [jnp.exp(x - m0) for x in sv0]
            ex1 = [jnp.exp(x - m1) for x in sv1]
            z = jnp.zeros((SUBLANES, LANES), F32)
            cnt = [jnp.zeros((SUBLANES, LANES), F32)] * TOPK
            for (p, q), cnd in zip(pq, cands):
                sel = cnd >= thr
                z = z + jnp.where(sel, ex0[p] * ex1[q], 0.0)
                cnt[p] = cnt[p] + jnp.where(sel, 1.0, 0.0)
            for k in range(TOPK):
                sv0_ref[k, :, sl] = sv0[k]
                sv1_ref[k, :, sl] = sv1[k]
                lcnt_ref[k, :, sl] = cnt[k]
            m0_ref[:, sl] = m0
            m1_ref[:, sl] = m1
            zinv_ref[:, sl] = 1.0 / z

            for hd in range(heads):
                hrow = slice(hd, hd + 1)
                s0t = s0_ref[hd, :, sl]
                s1t = s1_ref[hd, :, sl]
                lf = jnp.broadcast_to(lcnt_ref[0, hrow, sl], s0t.shape)
                r1 = jnp.zeros(s1t.shape, F32)
                for k in range(TOPK):
                    nxt = lcnt_ref[k + 1, hrow, sl] if k + 1 < TOPK else 0.0
                    lf = jnp.where(sv0_ref[k, hrow, sl] > s0t, nxt, lf)
                    r1 = jnp.where(sv1_ref[k, hrow, sl] > s1t, float(k + 1), r1)
                e0 = jnp.exp(s0t - m0_ref[hrow, sl]) * zinv_ref[hrow, sl]
                e1 = jnp.exp(s1t - m1_ref[hrow, sl])
                r1b = r1.astype(BF16)
                e1b = e1.astype(BF16)
                for g in range(n_keys // PACK):
                    r1_ref[hd, g, :, sl] = r1b[g * PACK:(g + 1) * PACK, :]
                    e1_ref[hd, g, :, sl] = e1b[g * PACK:(g + 1) * PACK, :]
                lfp_ref[hd, c] = lf.reshape(n_keys // SUBLANES, SUBLANES, LANES)
                e0p_ref[hd, c] = e0.reshape(n_keys // SUBLANES, SUBLANES, LANES)
            return carry

        lax.fori_loop(0, n_tiles, tile_body, 0)

    n_lt = T // W

    def xu_tile(half, st_ref, i, lt):
        ls = slice(lt * W, (lt + 1) * W)
        lo = half * E + i * n_keys
        st_ref[i, :, ls] = _dot(u_ref[lo:lo + n_keys, :], xnT_ref[:, ls])

    def bcast_row(ref, hd, lt, blk, i):
        parts = [jnp.broadcast_to(ref[hd, lt * (W // LANES) + k, blk][i:i + 1, :], (PACK, LANES)).astype(BF16)
                 for k in range(W // LANES)]
        return jnp.concatenate(parts, axis=-1)[None]

    zero_b = jnp.zeros((), BF16)

    def gate_tile(blk, st_ref, coef_ref, i, lt):
        ls = slice(lt * W, (lt + 1) * W)
        gsum = jnp.zeros((n_keys // PACK, PACK, W), BF16)
        for hd in range(heads):
            mask = r1_ref[hd, :, :, ls] < bcast_row(lfp_ref, hd, lt, blk, i)
            gsum = gsum + jnp.where(mask, e1_ref[hd, :, :, ls], zero_b) * bcast_row(e0p_ref, hd, lt, blk, i)
        hval = _gelu_tanh(st_ref[i, :, ls]).astype(BF16)
        for g in range(n_keys // PACK):
            lo = i * n_keys + g * PACK
            coef_ref[lo:lo + PACK, ls] = hval[g * PACK:(g + 1) * PACK, :] * gsum[g]

    def v_tile(half, coef_ref, lt):
        ls = slice(lt * W, (lt + 1) * W)
        accT_ref[:, ls] += _dot(vT_ref[:, half * E:(half + 1) * E], coef_ref[:, ls])

    def phase(xu=None, gate=None, vmm=None):
        for lt in range(n_lt):
            for i in range(i_per_step):
                if xu is not None:
                    xu_tile(*xu, i, lt)
                if gate is not None:
                    gate_tile(*gate, i, lt)
                if vmm is not None and i == i_per_step // 2 - 1:
                    v_tile(*vmm, lt)

    @pl.when(j == 0)
    def _first():
        prep()
        phase(xu=(0, sa_ref))
        phase(xu=(1, sb_ref), gate=(0, sa_ref, ca_ref))

    @pl.when(jnp.logical_and(j > 0, j < n_pairs))
    def _steady():
        phase(xu=(0, sa_ref), gate=(2 * j - 1, sb_ref, cb_ref), vmm=(0, ca_ref))
        phase(xu=(1, sb_ref), gate=(2 * j, sa_ref, ca_ref), vmm=(1, cb_ref))

    @pl.when(j == n_pairs)
    def _last():
        phase(gate=(2 * j - 1, sb_ref, cb_ref), vmm=(0, ca_ref))
        phase(vmm=(1, cb_ref))
        out_ref[...] = h_ref[...] + accT_ref[...].T


def _peer(h, gffn, wqT, keys, u_bf, vT_bf, *, T, E, W):
    N, D = h.shape
    heads = keys.shape[0] // 2
    n_keys = keys.shape[1]
    n_exp = u_bf.shape[0]
    n_pairs = n_exp // (2 * E)
    once = pl.Buffered(1)
    kern = functools.partial(_peer_kernel, T=T, E=E, W=W, heads=heads)
    return pl.pallas_call(
        kern,
        grid=(N // T, n_pairs + 1),
        in_specs=[
            pl.BlockSpec((T, D), lambda t, j: (t, 0)),
            pl.BlockSpec(gffn.shape, lambda t, j: (0, 0)),
            pl.BlockSpec(wqT.shape, lambda t, j: (0, 0), pipeline_mode=once),
            pl.BlockSpec(keys.shape, lambda t, j: (0, 0, 0), pipeline_mode=once),
            pl.BlockSpec((2 * E, D), lambda t, j: (jnp.minimum(j, n_pairs - 1), 0)),
            pl.BlockSpec((D, 2 * E), lambda t, j: (0, jnp.maximum(j - 1, 0))),
        ],
        out_specs=pl.BlockSpec((T, D), lambda t, j: (t, 0)),
        out_shape=jax.ShapeDtypeStruct((N, D), F32),
        scratch_shapes=[
            pltpu.VMEM((D, T), BF16),
            pltpu.VMEM((D, T), F32),
            pltpu.VMEM((max(heads, E // n_keys), n_keys, T), F32),
            pltpu.VMEM((max(heads, E // n_keys), n_keys, T), F32),
            pltpu.VMEM((E, T), BF16),
            pltpu.VMEM((E, T), BF16),
            pltpu.VMEM((heads, n_keys // PACK, PACK, T), BF16),
            pltpu.VMEM((heads, n_keys // PACK, PACK, T), BF16),
            pltpu.VMEM((heads, T // LANES, n_keys // SUBLANES, SUBLANES, LANES), F32),
            pltpu.VMEM((heads, T // LANES, n_keys // SUBLANES, SUBLANES, LANES), F32),
            pltpu.VMEM((TOPK, SUBLANES, T), F32),
            pltpu.VMEM((TOPK, SUBLANES, T), F32),
            pltpu.VMEM((TOPK, SUBLANES, T), F32),
            pltpu.VMEM((SUBLANES, T), F32),
            pltpu.VMEM((SUBLANES, T), F32),
            pltpu.VMEM((SUBLANES, T), F32),
        ],
        compiler_params=pltpu.CompilerParams(
            dimension_semantics=("arbitrary", "arbitrary"), vmem_limit_bytes=VMEM_LIMIT),
        name="peer",
    )(h, gffn, wqT, keys, u_bf, vT_bf)


def _ple_kernel(h_ref, p_ref, gple_ref, wgate_ref, wproj_ref, gfin_ref, out_ref, *, final):
    h = h_ref[...]
    hn = _rmsnorm(h, gple_ref[...]).astype(BF16)
    gate = _sigmoid(_dot(hn, wgate_ref[...]))
    proj = _dot(p_ref[...].astype(BF16), wproj_ref[...])
    h = h + proj * gate
    if final:
        h = _rmsnorm(h, gfin_ref[...])
    out_ref[...] = h


def _ple(h, p, gple, wgate, wproj, gfin, *, T, final):
    N, D = h.shape
    dp = p.shape[1]
    const2 = lambda t: (0, 0)
    return pl.pallas_call(
        functools.partial(_ple_kernel, final=final),
        grid=(N // T,),
        in_specs=[
            pl.BlockSpec((T, D), lambda t: (t, 0)),
            pl.BlockSpec((T, dp), lambda t: (t, 0)),
            pl.BlockSpec(gple.shape, const2),
            pl.BlockSpec(wgate.shape, const2),
            pl.BlockSpec(wproj.shape, const2),
            pl.BlockSpec(gfin.shape, const2),
        ],
        out_specs=pl.BlockSpec((T, D), lambda t: (t, 0)),
        out_shape=jax.ShapeDtypeStruct((N, D), F32),
        compiler_params=pltpu.CompilerParams(
            dimension_semantics=("arbitrary",), vmem_limit_bytes=VMEM_LIMIT),
        name="ple",
    )(h, p, gple, wgate, wproj, gfin)


def _block_diag(ws):
    g, n, _ = ws.shape
    out = jnp.zeros((g * n, g * n), ws.dtype)
    for k in range(g):
        out = out.at[k * n:(k + 1) * n, k * n:(k + 1) * n].set(ws[k])
    return out


def kernel(x_prompt, x_sample, state_pool, state_conv, p_prompt, p_sample, g_mix, w_in, w_pool, pool_scale, w_dw, b_dw, ln_g, ln_b, w_out, g_ffn, w_q, sub_keys, u_tab, v_tab, g_ple, w_ple_gate, w_ple_proj, g_final):
    depth = g_mix.shape[0]
    B, S, D = x_prompt.shape
    Bs, Ts, _ = x_sample.shape
    row = lambda v: v.reshape(1, -1)

    T_MIX = min(512, S)
    T_PEER = 512
    E_BLK = 1024
    W_GATE = 256
    BC = min(64, Bs)

    hp = x_prompt
    hs = jnp.transpose(x_sample, (1, 0, 2))
    pool_p, conv_p, pool_s, conv_s = [], [], [], []
    for i in range(depth):
        win = w_in[i].astype(BF16)
        wpool_bd = _block_diag(w_pool[i]).astype(BF16)
        wout = w_out[i].astype(BF16)
        mix_w = (row(g_mix[i]), win, wpool_bd, row(pool_scale[i]), w_dw[i], row(b_dw[i]), row(ln_g[i]),
                 row(ln_b[i]), wout)
        wqT = w_q[i].T.astype(BF16)
        heads, _, n_keys, d_half = sub_keys[i].shape
        keys = sub_keys[i].reshape(heads * 2, n_keys, d_half).astype(BF16)
        u_bf = u_tab[i].astype(BF16)
        vT_bf = v_tab[i].T.astype(BF16)
        wgate = w_ple_gate[i].astype(BF16)
        wproj = w_ple_proj[i].astype(BF16)
        final = i == depth - 1

        hp, npool, nconv = _mixer_prompt(hp, *mix_w, T=T_MIX)
        pool_p.append(npool[0])
        conv_p.append(nconv[0])
        hs, spool, sconv = _mixer_sample(
            hs, jnp.transpose(state_pool[i], (1, 0, 2)), jnp.transpose(state_conv[i], (1, 0, 2)), *mix_w, BC=BC)
        pool_s.append(jnp.transpose(spool, (1, 0, 2)))
        conv_s.append(jnp.transpose(sconv, (1, 0, 2)))

        hp2 = _peer(hp.reshape(B * S, D), row(g_ffn[i]), wqT, keys, u_bf, vT_bf, T=T_PEER, E=E_BLK, W=W_GATE)
        hs2 = _peer(hs.reshape(Ts * Bs, D), row(g_ffn[i]), wqT, keys, u_bf, vT_bf,
                    T=min(T_PEER, Ts * Bs), E=E_BLK, W=W_GATE)
        pp = p_prompt[i].reshape(B * S, -1)
        ps = jnp.transpose(p_sample[i], (1, 0, 2)).reshape(Ts * Bs, -1)
        hp = _ple(hp2, pp, row(g_ple[i]), wgate, wproj, row(g_final), T=T_PEER, final=final).reshape(B, S, D)
        hs = _ple(hs2, ps, row(g_ple[i]), wgate, wproj, row(g_final), T=min(T_PEER, Ts * Bs),
                  final=final).reshape(Ts, Bs, D)

    y_prompt = hp
    y_sample = jnp.transpose(hs, (1, 0, 2))
    return (y_prompt, y_sample, jnp.stack(pool_p), jnp.stack(conv_p), jnp.stack(pool_s), jnp.stack(conv_s))
```

```python
import functools

import jax
import jax.numpy as jnp
from jax import lax
from jax.experimental import pallas as pl
from jax.experimental.pallas import tpu as pltpu

EPS = 1e-6
PAST_LEN = 16384
POOL_WINDOWS = (2, 4, 8, 16)
TOPK = 16

LANES = 128
SUBLANES = 8
PACK = 16
VMEM_LIMIT = 56 * 1024 * 1024

BF16 = jnp.bfloat16
F32 = jnp.float32


def _dot(a, b):
    return jnp.dot(a, b, preferred_element_type=F32)


def _rmsnorm(x, g):
    return x * lax.rsqrt(jnp.mean(x * x, axis=-1, keepdims=True) + EPS) * g


def _sigmoid(x):
    return 1.0 / (1.0 + jnp.exp(-x))


def _gelu_tanh_bf16(x):
    c = 0.7978845608028654
    inner = (x * (x * x * (c * 0.044715) + c)).astype(BF16)
    hx = x.astype(BF16) * 0.5
    return hx * jnp.tanh(inner) + hx


def _conv_tail(y, bdw, lng, lnb):
    y = y + bdw
    mu = jnp.mean(y, axis=-1, keepdims=True)
    d = y - mu
    var = jnp.mean(d * d, axis=-1, keepdims=True)
    yn = d * lax.rsqrt(var + EPS) * lng + lnb
    return yn * _sigmoid(yn)


A_PAD = 16
U_PAD = 32


def _mixer_prompt_kernel(x_ref, gmix_ref, win_ref, wpool_ref, pscale_ref, wdw_ref, bdw_ref, lng_ref, lnb_ref,
                         wout_ref, h_ref, pool_out_ref, conv_out_ref, abuf, ubuf, *, T, d_pool, d_conv):
    s = pl.program_id(1)
    n_s = pl.num_programs(1)
    conv_width = wdw_ref.shape[0]
    pool_state = max(POOL_WINDOWS) - 1
    conv_state = conv_width - 1

    @pl.when(s == 0)
    def _():
        abuf[0:A_PAD, :] = jnp.zeros((A_PAD, d_pool), F32)
        ubuf[0:U_PAD, :] = jnp.zeros((U_PAD, d_conv), F32)
        ubuf[U_PAD + T:U_PAD + T + SUBLANES, :] = jnp.zeros((SUBLANES, d_conv), F32)

    x = x_ref[0]
    xn = _rmsnorm(x, gmix_ref[...]).astype(BF16)
    z = _dot(xn, win_ref[...])
    a = z[:, :d_pool]
    val = z[:, d_pool:d_pool + d_conv]
    gt = z[:, d_pool + d_conv:]
    u = val * _sigmoid(gt)
    abuf[A_PAD:A_PAD + T, :] = a
    ubuf[U_PAD:U_PAD + T, :] = u

    group = d_pool // len(POOL_WINDOWS)
    pos = s * T + lax.broadcasted_iota(jnp.int32, (T, group), 0)
    ys = []
    for g, w in enumerate(POOL_WINDOWS):
        lo = g * group
        win = abuf[A_PAD:A_PAD + T, lo:lo + group]
        for jj in range(1, w):
            win = win + abuf[A_PAD - jj:A_PAD - jj + T, lo:lo + group]
        cnt = jnp.minimum(pos + 1, w).astype(F32)
        ys.append(win / cnt - a[:, lo:lo + group])
    ypool = jnp.concatenate(ys, axis=-1).astype(BF16)
    ya = _dot(ypool, wpool_ref[...]) * pscale_ref[...]

    off = U_PAD - conv_state
    y = None
    for b in range(SUBLANES):
        q = None
        for a in range((off + conv_width - 1) // SUBLANES + 1):
            k = a * SUBLANES + b - off
            if 0 <= k < conv_width:
                term = ubuf[a * SUBLANES:a * SUBLANES + T + SUBLANES, :] * wdw_ref[k:k + 1, :]
                q = term if q is None else q + term
        part = q[b:b + T, :]
        y = part if y is None else y + part
    yb = _conv_tail(y, bdw_ref[...], lng_ref[...], lnb_ref[...])

    cat = jnp.concatenate([ya, yb], axis=-1).astype(BF16)
    h_ref[0] = x + _dot(cat, wout_ref[...])

    @pl.when(s == n_s - 1)
    def _():
        pool_out_ref[0, 0] = abuf[A_PAD + T - pool_state:A_PAD + T, :]
        conv_out_ref[0, 0] = ubuf[U_PAD + T - conv_state:U_PAD + T, :]

    abuf[0:A_PAD, :] = abuf[T:T + A_PAD, :]
    ubuf[0:U_PAD, :] = ubuf[T:T + U_PAD, :]


def _mixer_prompt(x, gmix, win, wpool_bd, pscale, wdw, bdw, lng, lnb, wout, *, T):
    B, S, D = x.shape
    d_pool = wpool_bd.shape[0]
    d_conv = wdw.shape[1]
    pool_state = max(POOL_WINDOWS) - 1
    conv_state = wdw.shape[0] - 1
    const2 = lambda b, s: (0, 0)
    kern = functools.partial(_mixer_prompt_kernel, T=T, d_pool=d_pool, d_conv=d_conv)
    return pl.pallas_call(
        kern,
        grid=(B, S // T),
        in_specs=[
            pl.BlockSpec((1, T, D), lambda b, s: (b, s, 0)),
            pl.BlockSpec(gmix.shape, const2),
            pl.BlockSpec(win.shape, const2),
            pl.BlockSpec(wpool_bd.shape, const2),
            pl.BlockSpec(pscale.shape, const2),
            pl.BlockSpec(wdw.shape, const2),
            pl.BlockSpec(bdw.shape, const2),
            pl.BlockSpec(lng.shape, const2),
            pl.BlockSpec(lnb.shape, const2),
            pl.BlockSpec(wout.shape, const2),
        ],
        out_specs=[
            pl.BlockSpec((1, T, D), lambda b, s: (b, s, 0)),
            pl.BlockSpec((1, 1, pool_state, d_pool), lambda b, s: (0, b, 0, 0)),
            pl.BlockSpec((1, 1, conv_state, d_conv), lambda b, s: (0, b, 0, 0)),
        ],
        out_shape=[
            jax.ShapeDtypeStruct((B, S, D), F32),
            jax.ShapeDtypeStruct((1, B, pool_state, d_pool), F32),
            jax.ShapeDtypeStruct((1, B, conv_state, d_conv), F32),
        ],
        scratch_shapes=[
            pltpu.VMEM((T + A_PAD, d_pool), F32),
            pltpu.VMEM((T + U_PAD + SUBLANES, d_conv), F32),
        ],
        compiler_params=pltpu.CompilerParams(
            dimension_semantics=("arbitrary", "arbitrary"), vmem_limit_bytes=VMEM_LIMIT),
        name="mixer_prompt",
    )(x, gmix, win, wpool_bd, pscale, wdw, bdw, lng, lnb, wout)


def _mixer_sample_kernel(x_ref, pool_ref, conv_ref, gmix_ref, win_ref, wpool_ref, pscale_ref, wdw_ref, bdw_ref,
                         lng_ref, lnb_ref, wout_ref, h_ref, pool_out_ref, conv_out_ref, *, d_pool, d_conv):
    Ts, BC, D = x_ref.shape
    P = pool_ref.shape[0]
    C = conv_ref.shape[0]
    conv_width = wdw_ref.shape[0]

    x = x_ref[...].reshape(Ts * BC, D)
    xn = _rmsnorm(x, gmix_ref[...]).astype(BF16)
    z = _dot(xn, win_ref[...])
    a = z[:, :d_pool]
    u = z[:, d_pool:d_pool + d_conv] * _sigmoid(z[:, d_pool + d_conv:])

    def full_a(k):
        return pool_ref[k] if k < P else a[(k - P) * BC:(k - P + 1) * BC, :]

    def full_u(k):
        return conv_ref[k] if k < C else u[(k - C) * BC:(k - C + 1) * BC, :]

    group = d_pool // len(POOL_WINDOWS)
    ya_rows, yb_rows = [], []
    for t in range(Ts):
        ys = []
        for g, w in enumerate(POOL_WINDOWS):
            lo = g * group
            win = full_a(P + t)[:, lo:lo + group]
            for jj in range(1, w):
                win = win + full_a(P + t - jj)[:, lo:lo + group]
            cnt = float(min(PAST_LEN + t + 1, w))
            ys.append(win / cnt - full_a(P + t)[:, lo:lo + group])
        ya_rows.append(jnp.concatenate(ys, axis=-1))
        y = full_u(t) * wdw_ref[0:1, :]
        for k in range(1, conv_width):
            y = y + full_u(t + k) * wdw_ref[k:k + 1, :]
        yb_rows.append(y)
    ypool = jnp.concatenate(ya_rows, axis=0).astype(BF16)
    ya = _dot(ypool, wpool_ref[...]) * pscale_ref[...]
    yb = _conv_tail(jnp.concatenate(yb_rows, axis=0), bdw_ref[...], lng_ref[...], lnb_ref[...])
    cat = jnp.concatenate([ya, yb], axis=-1).astype(BF16)
    h = x + _dot(cat, wout_ref[...])
    h_ref[...] = h.reshape(Ts, BC, D)
    for k in range(P):
        pool_out_ref[k] = full_a(Ts + k)
    for k in range(C):
        conv_out_ref[k] = full_u(Ts + k)


def _mixer_sample(x_tm, pool_tm, conv_tm, gmix, win, wpool_bd, pscale, wdw, bdw, lng, lnb, wout, *, BC):
    Ts, B, D = x_tm.shape
    P, _, d_pool = pool_tm.shape
    C, _, d_conv = conv_tm.shape
    const2 = lambda b: (0, 0)
    kern = functools.partial(_mixer_sample_kernel, d_pool=d_pool, d_conv=d_conv)
    return pl.pallas_call(
        kern,
        grid=(B // BC,),
        in_specs=[
            pl.BlockSpec((Ts, BC, D), lambda b: (0, b, 0)),
            pl.BlockSpec((P, BC, d_pool), lambda b: (0, b, 0)),
            pl.BlockSpec((C, BC, d_conv), lambda b: (0, b, 0)),
            pl.BlockSpec(gmix.shape, const2),
            pl.BlockSpec(win.shape, const2),
            pl.BlockSpec(wpool_bd.shape, const2),
            pl.BlockSpec(pscale.shape, const2),
            pl.BlockSpec(wdw.shape, const2),
            pl.BlockSpec(bdw.shape, const2),
            pl.BlockSpec(lng.shape, const2),
            pl.BlockSpec(lnb.shape, const2),
            pl.BlockSpec(wout.shape, const2),
        ],
        out_specs=[
            pl.BlockSpec((Ts, BC, D), lambda b: (0, b, 0)),
            pl.BlockSpec((P, BC, d_pool), lambda b: (0, b, 0)),
            pl.BlockSpec((C, BC, d_conv), lambda b: (0, b, 0)),
        ],
        out_shape=[
            jax.ShapeDtypeStruct((Ts, B, D), F32),
            jax.ShapeDtypeStruct((P, B, d_pool), F32),
            jax.ShapeDtypeStruct((C, B, d_conv), F32),
        ],
        compiler_params=pltpu.CompilerParams(
            dimension_semantics=("arbitrary",), vmem_limit_bytes=VMEM_LIMIT),
        name="mixer_sample",
    )(x_tm, pool_tm, conv_tm, gmix, win, wpool_bd, pscale, wdw, bdw, lng, lnb, wout)


def _batcher_pairs(n):
    pairs = []
    t = (n - 1).bit_length()
    p = 1 << (t - 1)
    while p > 0:
        q = 1 << (t - 1)
        r = 0
        d = p
        while d > 0:
            for i in range(n - d):
                if (i & p) == r:
                    pairs.append((i, i + d))
            d = q - p
            q >>= 1
            r = p
        p >>= 1
    return pairs


_SORT16 = _batcher_pairs(TOPK)


def _top16_sorted(s):
    n = s.shape[0] // SUBLANES
    v = [s[k * SUBLANES:(k + 1) * SUBLANES, :] for k in range(n)]
    for i, j in _SORT16:
        hi = jnp.maximum(v[i], v[j])
        lo = jnp.minimum(v[i], v[j])
        v[i], v[j] = hi, lo
    for sh in (4, 2, 1):
        b = [pltpu.roll(x, sh, 0) for x in v]
        v = [jnp.maximum(v[k], b[n - 1 - k]) for k in range(n)]
        st = n // 2
        while st >= 1:
            for k in range(n):
                if not (k & st):
                    hi = jnp.maximum(v[k], v[k + st])
                    lo = jnp.minimum(v[k], v[k + st])
                    v[k], v[k + st] = hi, lo
            st //= 2
    return v


def _kth_largest(cands, k):
    def tree_max(xs):
        xs = list(xs)
        while len(xs) > 1:
            nxt = [jnp.maximum(xs[i], xs[i + 1]) for i in range(0, len(xs) - 1, 2)]
            if len(xs) % 2:
                nxt.append(xs[-1])
            xs = nxt
        return xs[0]

    cur = list(cands)
    for _ in range(k - 1):
        m = tree_max(cur)
        cur = [jnp.where(c == m, -jnp.inf, c) for c in cur]
    return tree_max(cur)


def _peer_kernel(h_ref, gffn_ref, wqT_ref, keys_ref, u_ref, vT_ref, out_ref,
                 xnT_ref, accT_ref, sa_ref, sb_ref, ca_ref, cb_ref, r1_ref, e1_ref, lfp_ref, e0p_ref,
                 sv0_ref, sv1_ref, lcnt_ref, m0_ref, m1_ref, zinv_ref,
                 *, T, E, W, heads):
    j = pl.program_id(1)
    n_pairs = pl.num_programs(1) - 1
    n_keys = keys_ref.shape[1]
    d_half = keys_ref.shape[2]
    n_tiles = T // LANES
    i_per_step = E // n_keys
    assert i_per_step == SUBLANES and n_keys == 8 * PACK and heads <= sa_ref.shape[0]
    s0_ref, s1_ref = sa_ref, sb_ref

    def prep():
        xn = _rmsnorm(h_ref[...], gffn_ref[...])
        xnT = xn.T.astype(BF16)
        xnT_ref[...] = xnT
        accT_ref[...] = jnp.zeros(accT_ref.shape, F32)
        for hd in range(heads):
            qT = _dot(wqT_ref[hd * 2 * d_half:(hd + 1) * 2 * d_half, :], xnT).astype(BF16)
            s0_ref[hd] = _dot(keys_ref[2 * hd], qT[:d_half, :])
            s1_ref[hd] = _dot(keys_ref[2 * hd + 1], qT[d_half:, :])

        sub = lax.broadcasted_iota(jnp.int32, (SUBLANES, LANES), 0)
        pq = [(p, q) for p in range(TOPK) for q in range(TOPK) if (p + 1) * (q + 1) <= TOPK]

        def tile_body(c, carry):
            sl = pl.ds(pl.multiple_of(c * LANES, LANES), LANES)
            sv0 = [jnp.zeros((SUBLANES, LANES), F32)] * TOPK
            sv1 = [jnp.zeros((SUBLANES, LANES), F32)] * TOPK
            for hd in range(heads):
                t0 = _top16_sorted(s0_ref[hd, :, sl])
                t1 = _top16_sorted(s1_ref[hd, :, sl])
                sv0 = [jnp.where(sub == hd, t0[k], sv0[k]) for k in range(TOPK)]
                sv1 = [jnp.where(sub == hd, t1[k], sv1[k]) for k in range(TOPK)]
            cands = [sv0[p] + sv1[q] for p, q in pq]
            thr = _kth_largest(cands, TOPK)
            m0 = sv0[0]
            m1 = sv1[0]
            ex0 = [jnp.exp(x - m0) for x in sv0]
            ex1 = [jnp.exp(x - m1) for x in sv1]
            z = jnp.zeros((SUBLANES, LANES), F32)
            cnt = [jnp.zeros((SUBLANES, LANES), F32)] * TOPK
            for (p, q), cnd in zip(pq, cands):
                sel = cnd >= thr
                z = z + jnp.where(sel, ex0[p] * ex1[q], 0.0)
                cnt[p] = cnt[p] + jnp.where(sel, 1.0, 0.0)
            for k in range(TOPK):
                sv0_ref[k, :, sl] = sv0[k]
                sv1_ref[k, :, sl] = sv1[k]
                lcnt_ref[k, :, sl] = cnt[k]
            m0_ref[:, sl] = m0
            m1_ref[:, sl] = m1
            zinv_ref[:, sl] = 1.0 / z

            for hd in range(heads):
                hrow = slice(hd, hd + 1)
                s0t = s0_ref[hd, :, sl]
                s1t = s1_ref[hd, :, sl]
                lf = jnp.broadcast_to(lcnt_ref[0, hrow, sl], s0t.shape)
                r1 = jnp.zeros(s1t.shape, F32)
                for k in range(TOPK):
                    nxt = lcnt_ref[k + 1, hrow, sl] if k + 1 < TOPK else 0.0
                    lf = jnp.where(sv0_ref[k, hrow, sl] > s0t, nxt, lf)
                    r1 = jnp.where(sv1_ref[k, hrow, sl] > s1t, float(k + 1), r1)
                e0 = jnp.exp(s0t - m0_ref[hrow, sl]) * zinv_ref[hrow, sl]
                e1 = jnp.exp(s1t - m1_ref[hrow, sl])
                r1b = r1.astype(BF16)
                e1b = e1.astype(BF16)
                for g in range(n_keys // PACK):
                    r1_ref[hd, g, :, sl] = r1b[g * PACK:(g + 1) * PACK, :]
                    e1_ref[hd, g, :, sl] = e1b[g * PACK:(g + 1) * PACK, :]
                lfp_ref[hd, c] = lf.reshape(n_keys // SUBLANES, SUBLANES, LANES)
                e0p_ref[hd, c] = e0.reshape(n_keys // SUBLANES, SUBLANES, LANES)
            return carry

        lax.fori_loop(0, n_tiles, tile_body, 0)

    n_lt = T // W

    def xu_tile(half, st_ref, i, lt):
        ls = slice(lt * W, (lt + 1) * W)
        lo = half * E + i * n_keys
        st_ref[i, :, ls] = _dot(u_ref[lo:lo + n_keys, :], xnT_ref[:, ls])

    def bcast_row(ref, hd, lt, blk, i):
        parts = [jnp.broadcast_to(ref[hd, lt * (W // LANES) + k, blk][i:i + 1, :], (PACK, LANES)).astype(BF16)
                 for k in range(W // LANES)]
        return jnp.concatenate(parts, axis=-1)[None]

    zero_b = jnp.zeros((), BF16)

    def gate_tile(blk, st_ref, coef_ref, i, lt):
        ls = slice(lt * W, (lt + 1) * W)
        gsum = jnp.zeros((n_keys // PACK, PACK, W), BF16)
        for hd in range(heads):
            mask = r1_ref[hd, :, :, ls] < bcast_row(lfp_ref, hd, lt, blk, i)
            gsum = gsum + jnp.where(mask, e1_ref[hd, :, :, ls], zero_b) * bcast_row(e0p_ref, hd, lt, blk, i)
        hval = _gelu_tanh_bf16(st_ref[i, :, ls])
        for g in range(n_keys // PACK):
            lo = i * n_keys + g * PACK
            coef_ref[lo:lo + PACK, ls] = hval[g * PACK:(g + 1) * PACK, :] * gsum[g]

    def v_tile(half, coef_ref, lt):
        ls = slice(lt * W, (lt + 1) * W)
        accT_ref[:, ls] += _dot(vT_ref[:, half * E:(half + 1) * E], coef_ref[:, ls])

    def phase(xu=None, gate=None, vmm=None):
        for lt in range(n_lt):
            for i in range(i_per_step):
                if xu is not None:
                    xu_tile(*xu, i, lt)
                if gate is not None:
                    gate_tile(*gate, i, lt)
                if vmm is not None and i == i_per_step // 2 - 1:
                    v_tile(*vmm, lt)

    @pl.when(j == 0)
    def _first():
        prep()
        phase(xu=(0, sa_ref))
        phase(xu=(1, sb_ref), gate=(0, sa_ref, ca_ref))

    @pl.when(jnp.logical_and(j > 0, j < n_pairs))
    def _steady():
        phase(xu=(0, sa_ref), gate=(2 * j - 1, sb_ref, cb_ref), vmm=(0, ca_ref))
        phase(xu=(1, sb_ref), gate=(2 * j, sa_ref, ca_ref), vmm=(1, cb_ref))

    @pl.when(j == n_pairs)
    def _last():
        phase(gate=(2 * j - 1, sb_ref, cb_ref), vmm=(0, ca_ref))
        phase(vmm=(1, cb_ref))
        out_ref[...] = h_ref[...] + accT_ref[...].T


def _peer(h, gffn, wqT, keys, u_bf, vT_bf, *, T, E, W):
    N, D = h.shape
    heads = keys.shape[0] // 2
    n_keys = keys.shape[1]
    n_exp = u_bf.shape[0]
    n_pairs = n_exp // (2 * E)
    once = pl.Buffered(1)
    kern = functools.partial(_peer_kernel, T=T, E=E, W=W, heads=heads)
    return pl.pallas_call(
        kern,
        grid=(N // T, n_pairs + 1),
        in_specs=[
            pl.BlockSpec((T, D), lambda t, j: (t, 0)),
            pl.BlockSpec(gffn.shape, lambda t, j: (0, 0)),
            pl.BlockSpec(wqT.shape, lambda t, j: (0, 0), pipeline_mode=once),
            pl.BlockSpec(keys.shape, lambda t, j: (0, 0, 0), pipeline_mode=once),
            pl.BlockSpec((2 * E, D), lambda t, j: (jnp.minimum(j, n_pairs - 1), 0)),
            pl.BlockSpec((D, 2 * E), lambda t, j: (0, jnp.maximum(j - 1, 0))),
        ],
        out_specs=pl.BlockSpec((T, D), lambda t, j: (t, 0)),
        out_shape=jax.ShapeDtypeStruct((N, D), F32),
        scratch_shapes=[
            pltpu.VMEM((D, T), BF16),
            pltpu.VMEM((D, T), F32),
            pltpu.VMEM((max(heads, E // n_keys), n_keys, T), F32),
            pltpu.VMEM((max(heads, E // n_keys), n_keys, T), F32),
            pltpu.VMEM((E, T), BF16),
            pltpu.VMEM((E, T), BF16),
            pltpu.VMEM((heads, n_keys // PACK, PACK, T), BF16),
            pltpu.VMEM((heads, n_keys // PACK, PACK, T), BF16),
            pltpu.VMEM((heads, T // LANES, n_keys // SUBLANES, SUBLANES, LANES), F32),
            pltpu.VMEM((heads, T // LANES, n_keys // SUBLANES, SUBLANES, LANES), F32),
            pltpu.VMEM((TOPK, SUBLANES, T), F32),
            pltpu.VMEM((TOPK, SUBLANES, T), F32),
            pltpu.VMEM((TOPK, SUBLANES, T), F32),
            pltpu.VMEM((SUBLANES, T), F32),
            pltpu.VMEM((SUBLANES, T), F32),
            pltpu.VMEM((SUBLANES, T), F32),
        ],
        compiler_params=pltpu.CompilerParams(
            dimension_semantics=("arbitrary", "arbitrary"), vmem_limit_bytes=VMEM_LIMIT),
        name="peer",
    )(h, gffn, wqT, keys, u_bf, vT_bf)


def _ple_kernel(h_ref, p_ref, gple_ref, wgate_ref, wproj_ref, gfin_ref, out_ref, *, final):
    h = h_ref[...]
    hn = _rmsnorm(h, gple_ref[...]).astype(BF16)
    gate = _sigmoid(_dot(hn, wgate_ref[...]))
    proj = _dot(p_ref[...].astype(BF16), wproj_ref[...])
    h = h + proj * gate
    if final:
        h = _rmsnorm(h, gfin_ref[...])
    out_ref[...] = h


def _ple(h, p, gple, wgate, wproj, gfin, *, T, final):
    N, D = h.shape
    dp = p.shape[1]
    const2 = lambda t: (0, 0)
    return pl.pallas_call(
        functools.partial(_ple_kernel, final=final),
        grid=(N // T,),
        in_specs=[
            pl.BlockSpec((T, D), lambda t: (t, 0)),
            pl.BlockSpec((T, dp), lambda t: (t, 0)),
            pl.BlockSpec(gple.shape, const2),
            pl.BlockSpec(wgate.shape, const2),
            pl.BlockSpec(wproj.shape, const2),
            pl.BlockSpec(gfin.shape, const2),
        ],
        out_specs=pl.BlockSpec((T, D), lambda t: (t, 0)),
        out_shape=jax.ShapeDtypeStruct((N, D), F32),
        compiler_params=pltpu.CompilerParams(
            dimension_semantics=("arbitrary",), vmem_limit_bytes=VMEM_LIMIT),
        name="ple",
    )(h, p, gple, wgate, wproj, gfin)


def _block_diag(ws):
    g, n, _ = ws.shape
    out = jnp.zeros((g * n, g * n), ws.dtype)
    for k in range(g):
        out = out.at[k * n:(k + 1) * n, k * n:(k + 1) * n].set(ws[k])
    return out


def kernel(x_prompt, x_sample, state_pool, state_conv, p_prompt, p_sample, g_mix, w_in, w_pool, pool_scale, w_dw, b_dw, ln_g, ln_b, w_out, g_ffn, w_q, sub_keys, u_tab, v_tab, g_ple, w_ple_gate, w_ple_proj, g_final):
    depth = g_mix.shape[0]
    B, S, D = x_prompt.shape
    Bs, Ts, _ = x_sample.shape
    row = lambda v: v.reshape(1, -1)

    T_MIX = min(512, S)
    T_PEER = 512
    E_BLK = 1024
    W_GATE = 256
    BC = min(64, Bs)

    hp = x_prompt
    hs = jnp.transpose(x_sample, (1, 0, 2))
    pool_p, conv_p, pool_s, conv_s = [], [], [], []
    for i in range(depth):
        win = w_in[i].astype(BF16)
        wpool_bd = _block_diag(w_pool[i]).astype(BF16)
        wout = w_out[i].astype(BF16)
        mix_w = (row(g_mix[i]), win, wpool_bd, row(pool_scale[i]), w_dw[i], row(b_dw[i]), row(ln_g[i]),
                 row(ln_b[i]), wout)
        wqT = w_q[i].T.astype(BF16)
        heads, _, n_keys, d_half = sub_keys[i].shape
        keys = sub_keys[i].reshape(heads * 2, n_keys, d_half).astype(BF16)
        u_bf = u_tab[i].astype(BF16)
        vT_bf = v_tab[i].T.astype(BF16)
        wgate = w_ple_gate[i].astype(BF16)
        wproj = w_ple_proj[i].astype(BF16)
        final = i == depth - 1

        hp, npool, nconv = _mixer_prompt(hp, *mix_w, T=T_MIX)
        pool_p.append(npool[0])
        conv_p.append(nconv[0])
        hs, spool, sconv = _mixer_sample(
            hs, jnp.transpose(state_pool[i], (1, 0, 2)), jnp.transpose(state_conv[i], (1, 0, 2)), *mix_w, BC=BC)
        pool_s.append(jnp.transpose(spool, (1, 0, 2)))
        conv_s.append(jnp.transpose(sconv, (1, 0, 2)))

        hp2 = _peer(hp.reshape(B * S, D), row(g_ffn[i]), wqT, keys, u_bf, vT_bf, T=T_PEER, E=E_BLK, W=W_GATE)
        hs2 = _peer(hs.reshape(Ts * Bs, D), row(g_ffn[i]), wqT, keys, u_bf, vT_bf,
                    T=min(T_PEER, Ts * Bs), E=E_BLK, W=W_GATE)
        pp = p_prompt[i].reshape(B * S, -1)
        ps = jnp.transpose(p_sample[i], (1, 0, 2)).reshape(Ts * Bs, -1)
        hp = _ple(hp2, pp, row(g_ple[i]), wgate, wproj, row(g_final), T=T_PEER, final=final).reshape(B, S, D)
        hs = _ple(hs2, ps, row(g_ple[i]), wgate, wproj, row(g_final), T=min(T_PEER, Ts * Bs),
                  final=final).reshape(Ts, Bs, D)

    y_prompt = hp
    y_sample = jnp.transpose(hs, (1, 0, 2))
    return (y_prompt, y_sample, jnp.stack(pool_p), jnp.stack(conv_p), jnp.stack(pool_s), jnp.stack(conv_s))
```

```python
import functools

import jax
import jax.numpy as jnp
from jax import lax
from jax.experimental import pallas as pl
from jax.experimental.pallas import tpu as pltpu

EPS = 1e-6
PAST_LEN = 16384
POOL_WINDOWS = (2, 4, 8, 16)
TOPK = 16

LANES = 128
SUBLANES = 8
PACK = 16
VMEM_LIMIT = 56 * 1024 * 1024

BF16 = jnp.bfloat16
F32 = jnp.float32


def _dot(a, b):
    return jnp.dot(a, b, preferred_element_type=F32)


def _rmsnorm(x, g):
    return x * lax.rsqrt(jnp.mean(x * x, axis=-1, keepdims=True) + EPS) * g


def _sigmoid(x):
    return 1.0 / (1.0 + jnp.exp(-x))


def _gelu_tanh_bf16(x):
    c = 0.7978845608028654
    inner = (x * (x * x * (c * 0.044715) + c)).astype(BF16)
    hx = x.astype(BF16) * 0.5
    return hx * jnp.tanh(inner) + hx


def _conv_tail(y, bdw, lng, lnb):
    y = y + bdw
    mu = jnp.mean(y, axis=-1, keepdims=True)
    d = y - mu
    var = jnp.mean(d * d, axis=-1, keepdims=True)
    yn = d * lax.rsqrt(var + EPS) * lng + lnb
    return yn * _sigmoid(yn)


A_PAD = 16
U_PAD = 32


def _mixer_prompt_kernel(x_ref, gmix_ref, win_ref, wpool_ref, pscale_ref, wdw_ref, bdw_ref, lng_ref, lnb_ref,
                         wout_ref, h_ref, pool_out_ref, conv_out_ref, abuf, ubuf, *, T, d_pool, d_conv):
    s = pl.program_id(1)
    n_s = pl.num_programs(1)
    conv_width = wdw_ref.shape[0]
    pool_state = max(POOL_WINDOWS) - 1
    conv_state = conv_width - 1

    @pl.when(s == 0)
    def _():
        abuf[0:A_PAD, :] = jnp.zeros((A_PAD, d_pool), F32)
        ubuf[0:U_PAD, :] = jnp.zeros((U_PAD, d_conv), F32)
        ubuf[U_PAD + T:U_PAD + T + SUBLANES, :] = jnp.zeros((SUBLANES, d_conv), F32)

    x = x_ref[0]
    xn = _rmsnorm(x, gmix_ref[...]).astype(BF16)
    z = _dot(xn, win_ref[...])
    a = z[:, :d_pool]
    val = z[:, d_pool:d_pool + d_conv]
    gt = z[:, d_pool + d_conv:]
    u = val * _sigmoid(gt)
    abuf[A_PAD:A_PAD + T, :] = a
    ubuf[U_PAD:U_PAD + T, :] = u

    group = d_pool // len(POOL_WINDOWS)
    pos = s * T + lax.broadcasted_iota(jnp.int32, (T, group), 0)
    ys = []
    for g, w in enumerate(POOL_WINDOWS):
        lo = g * group
        win = abuf[A_PAD:A_PAD + T, lo:lo + group]
        for jj in range(1, w):
            win = win + abuf[A_PAD - jj:A_PAD - jj + T, lo:lo + group]
        cnt = jnp.minimum(pos + 1, w).astype(F32)
        ys.append(win / cnt - a[:, lo:lo + group])
    ypool = jnp.concatenate(ys, axis=-1).astype(BF16)
    ya = _dot(ypool, wpool_ref[...]) * pscale_ref[...]

    off = U_PAD - conv_state
    y = None
    for b in range(SUBLANES):
        q = None
        for a in range((off + conv_width - 1) // SUBLANES + 1):
            k = a * SUBLANES + b - off
            if 0 <= k < conv_width:
                term = ubuf[a * SUBLANES:a * SUBLANES + T + SUBLANES, :] * wdw_ref[k:k + 1, :]
                q = term if q is None else q + term
        part = q[b:b + T, :]
        y = part if y is None else y + part
    yb = _conv_tail(y, bdw_ref[...], lng_ref[...], lnb_ref[...])

    cat = jnp.concatenate([ya, yb], axis=-1).astype(BF16)
    h_ref[0] = x + _dot(cat, wout_ref[...])

    @pl.when(s == n_s - 1)
    def _():
        pool_out_ref[0, 0] = abuf[A_PAD + T - pool_state:A_PAD + T, :]
        conv_out_ref[0, 0] = ubuf[U_PAD + T - conv_state:U_PAD + T, :]

    abuf[0:A_PAD, :] = abuf[T:T + A_PAD, :]
    ubuf[0:U_PAD, :] = ubuf[T:T + U_PAD, :]


def _mixer_prompt(x, gmix, win, wpool_bd, pscale, wdw, bdw, lng, lnb, wout, *, T):
    B, S, D = x.shape
    d_pool = wpool_bd.shape[0]
    d_conv = wdw.shape[1]
    pool_state = max(POOL_WINDOWS) - 1
    conv_state = wdw.shape[0] - 1
    const2 = lambda b, s: (0, 0)
    kern = functools.partial(_mixer_prompt_kernel, T=T, d_pool=d_pool, d_conv=d_conv)
    return pl.pallas_call(
        kern,
        grid=(B, S // T),
        in_specs=[
            pl.BlockSpec((1, T, D), lambda b, s: (b, s, 0)),
            pl.BlockSpec(gmix.shape, const2),
            pl.BlockSpec(win.shape, const2),
            pl.BlockSpec(wpool_bd.shape, const2),
            pl.BlockSpec(pscale.shape, const2),
            pl.BlockSpec(wdw.shape, const2),
            pl.BlockSpec(bdw.shape, const2),
            pl.BlockSpec(lng.shape, const2),
            pl.BlockSpec(lnb.shape, const2),
            pl.BlockSpec(wout.shape, const2),
        ],
        out_specs=[
            pl.BlockSpec((1, T, D), lambda b, s: (b, s, 0)),
            pl.BlockSpec((1, 1, pool_state, d_pool), lambda b, s: (0, b, 0, 0)),
            pl.BlockSpec((1, 1, conv_state, d_conv), lambda b, s: (0, b, 0, 0)),
        ],
        out_shape=[
            jax.ShapeDtypeStruct((B, S, D), F32),
            jax.ShapeDtypeStruct((1, B, pool_state, d_pool), F32),
            jax.ShapeDtypeStruct((1, B, conv_state, d_conv), F32),
        ],
        scratch_shapes=[
            pltpu.VMEM((T + A_PAD, d_pool), F32),
            pltpu.VMEM((T + U_PAD + SUBLANES, d_conv), F32),
        ],
        compiler_params=pltpu.CompilerParams(
            dimension_semantics=("arbitrary", "arbitrary"), vmem_limit_bytes=VMEM_LIMIT),
        name="mixer_prompt",
    )(x, gmix, win, wpool_bd, pscale, wdw, bdw, lng, lnb, wout)


def _mixer_sample_kernel(x_ref, pool_ref, conv_ref, gmix_ref, win_ref, wpool_ref, pscale_ref, wdw_ref, bdw_ref,
                         lng_ref, lnb_ref, wout_ref, h_ref, pool_out_ref, conv_out_ref, *, d_pool, d_conv):
    Ts, BC, D = x_ref.shape
    P = pool_ref.shape[0]
    C = conv_ref.shape[0]
    conv_width = wdw_ref.shape[0]

    x = x_ref[...].reshape(Ts * BC, D)
    xn = _rmsnorm(x, gmix_ref[...]).astype(BF16)
    z = _dot(xn, win_ref[...])
    a = z[:, :d_pool]
    u = z[:, d_pool:d_pool + d_conv] * _sigmoid(z[:, d_pool + d_conv:])

    def full_a(k):
        return pool_ref[k] if k < P else a[(k - P) * BC:(k - P + 1) * BC, :]

    def full_u(k):
        return conv_ref[k] if k < C else u[(k - C) * BC:(k - C + 1) * BC, :]

    group = d_pool // len(POOL_WINDOWS)
    ya_rows, yb_rows = [], []
    for t in range(Ts):
        ys = []
        for g, w in enumerate(POOL_WINDOWS):
            lo = g * group
            win = full_a(P + t)[:, lo:lo + group]
            for jj in range(1, w):
                win = win + full_a(P + t - jj)[:, lo:lo + group]
            cnt = float(min(PAST_LEN + t + 1, w))
            ys.append(win / cnt - full_a(P + t)[:, lo:lo + group])
        ya_rows.append(jnp.concatenate(ys, axis=-1))
        y = full_u(t) * wdw_ref[0:1, :]
        for k in range(1, conv_width):
            y = y + full_u(t + k) * wdw_ref[k:k + 1, :]
        yb_rows.append(y)
    ypool = jnp.concatenate(ya_rows, axis=0).astype(BF16)
    ya = _dot(ypool, wpool_ref[...]) * pscale_ref[...]
    yb = _conv_tail(jnp.concatenate(yb_rows, axis=0), bdw_ref[...], lng_ref[...], lnb_ref[...])
    cat = jnp.concatenate([ya, yb], axis=-1).astype(BF16)
    h = x + _dot(cat, wout_ref[...])
    h_ref[...] = h.reshape(Ts, BC, D)
    for k in range(P):
        pool_out_ref[k] = full_a(Ts + k)
    for k in range(C):
        conv_out_ref[k] = full_u(Ts + k)


def _mixer_sample(x_tm, pool_tm, conv_tm, gmix, win, wpool_bd, pscale, wdw, bdw, lng, lnb, wout, *, BC):
    Ts, B, D = x_tm.shape
    P, _, d_pool = pool_tm.shape
    C, _, d_conv = conv_tm.shape
    const2 = lambda b: (0, 0)
    kern = functools.partial(_mixer_sample_kernel, d_pool=d_pool, d_conv=d_conv)
    return pl.pallas_call(
        kern,
        grid=(B // BC,),
        in_specs=[
            pl.BlockSpec((Ts, BC, D), lambda b: (0, b, 0)),
            pl.BlockSpec((P, BC, d_pool), lambda b: (0, b, 0)),
            pl.BlockSpec((C, BC, d_conv), lambda b: (0, b, 0)),
            pl.BlockSpec(gmix.shape, const2),
            pl.BlockSpec(win.shape, const2),
            pl.BlockSpec(wpool_bd.shape, const2),
            pl.BlockSpec(pscale.shape, const2),
            pl.BlockSpec(wdw.shape, const2),
            pl.BlockSpec(bdw.shape, const2),
            pl.BlockSpec(lng.shape, const2),
            pl.BlockSpec(lnb.shape, const2),
            pl.BlockSpec(wout.shape, const2),
        ],
        out_specs=[
            pl.BlockSpec((Ts, BC, D), lambda b: (0, b, 0)),
            pl.BlockSpec((P, BC, d_pool), lambda b: (0, b, 0)),
            pl.BlockSpec((C, BC, d_conv), lambda b: (0, b, 0)),
        ],
        out_shape=[
            jax.ShapeDtypeStruct((Ts, B, D), F32),
            jax.ShapeDtypeStruct((P, B, d_pool), F32),
            jax.ShapeDtypeStruct((C, B, d_conv), F32),
        ],
        compiler_params=pltpu.CompilerParams(
            dimension_semantics=("arbitrary",), vmem_limit_bytes=VMEM_LIMIT),
        name="mixer_sample",
    )(x_tm, pool_tm, conv_tm, gmix, win, wpool_bd, pscale, wdw, bdw, lng, lnb, wout)


def _batcher_pairs(n):
    pairs = []
    t = (n - 1).bit_length()
    p = 1 << (t - 1)
    while p > 0:
        q = 1 << (t - 1)
        r = 0
        d = p
        while d > 0:
            for i in range(n - d):
                if (i & p) == r:
                    pairs.append((i, i + d))
            d = q - p
            q >>= 1
            r = p
        p >>= 1
    return pairs


_SORT16 = _batcher_pairs(TOPK)


def _top16_sorted(s):
    n = s.shape[0] // SUBLANES
    v = [s[k * SUBLANES:(k + 1) * SUBLANES, :] for k in range(n)]
    for i, j in _SORT16:
        hi = jnp.maximum(v[i], v[j])
        lo = jnp.minimum(v[i], v[j])
        v[i], v[j] = hi, lo
    for sh in (4, 2, 1):
        b = [pltpu.roll(x, sh, 0) for x in v]
        v = [jnp.maximum(v[k], b[n - 1 - k]) for k in range(n)]
        st = n // 2
        while st >= 1:
            for k in range(n):
                if not (k & st):
                    hi = jnp.maximum(v[k], v[k + st])
                    lo = jnp.minimum(v[k], v[k + st])
                    v[k], v[k + st] = hi, lo
            st //= 2
    return v


def _kth_largest(cands, k):
    def tree_max(xs):
        xs = list(xs)
        while len(xs) > 1:
            nxt = [jnp.maximum(xs[i], xs[i + 1]) for i in range(0, len(xs) - 1, 2)]
            if len(xs) % 2:
                nxt.append(xs[-1])
            xs = nxt
        return xs[0]

    cur = list(cands)
    for _ in range(k - 1):
        m = tree_max(cur)
        cur = [jnp.where(c == m, -jnp.inf, c) for c in cur]
    return tree_max(cur)


def _peer_kernel(h_ref, gffn_ref, wqT_ref, keys_ref, u_ref, vT_ref, out_ref,
                 xnT_ref, accT_ref, sa_ref, sb_ref, ca_ref, cb_ref, s0_ref, s1_ref, r1_ref, e1_ref, lfp_ref, e0p_ref,
                 sv0_ref, sv1_ref, lcnt_ref, m0_ref, m1_ref, zinv_ref,
                 *, T, E, W, heads):
    j = pl.program_id(1)
    n_pairs = pl.num_programs(1) - 1
    n_keys = keys_ref.shape[1]
    d_half = keys_ref.shape[2]
    n_tiles = T // LANES
    i_per_step = E // n_keys
    n_lt = T // W
    assert i_per_step == SUBLANES and n_keys == 8 * PACK

    def xu_tile(half, st_ref, i, lt):
        ls = slice(lt * W, (lt + 1) * W)
        lo = half * E + i * n_keys
        rows = slice(lo, lo + n_keys) if isinstance(lo, int) else pl.ds(pl.multiple_of(lo, n_keys), n_keys)
        st_ref[i, :, ls] = _dot(u_ref[rows, :], xnT_ref[:, ls])

    def first_step():
        xn = _rmsnorm(h_ref[...], gffn_ref[...])
        xnT = xn.T.astype(BF16)
        xnT_ref[...] = xnT
        accT_ref[...] = jnp.zeros(accT_ref.shape, F32)
        for hd in range(heads):
            qT = _dot(wqT_ref[hd * 2 * d_half:(hd + 1) * 2 * d_half, :], xnT).astype(BF16)
            s0_ref[hd] = _dot(keys_ref[2 * hd], qT[:d_half, :])
            s1_ref[hd] = _dot(keys_ref[2 * hd + 1], qT[d_half:, :])

        sub = lax.broadcasted_iota(jnp.int32, (SUBLANES, LANES), 0)
        pq = [(p, q) for p in range(TOPK) for q in range(TOPK) if (p + 1) * (q + 1) <= TOPK]
        xu_per_tile = i_per_step // n_tiles

        def tile_body(c, carry):
            sl = pl.ds(pl.multiple_of(c * LANES, LANES), LANES)
            xu_queue = [functools.partial(xu_tile, half, st, c * xu_per_tile + k, lt)
                        for half, st in ((0, sa_ref), (1, sb_ref)) for lt in range(n_lt) for k in range(xu_per_tile)]
            sv0 = [jnp.zeros((SUBLANES, LANES), F32)] * TOPK
            sv1 = [jnp.zeros((SUBLANES, LANES), F32)] * TOPK
            for hd in range(heads):
                if xu_queue:
                    xu_queue.pop(0)()
                t0 = _top16_sorted(s0_ref[hd, :, sl])
                t1 = _top16_sorted(s1_ref[hd, :, sl])
                sv0 = [jnp.where(sub == hd, t0[k], sv0[k]) for k in range(TOPK)]
                sv1 = [jnp.where(sub == hd, t1[k], sv1[k]) for k in range(TOPK)]
            cands = [sv0[p] + sv1[q] for p, q in pq]
            thr = _kth_largest(cands, TOPK)
            m0 = sv0[0]
            m1 = sv1[0]
            ex0 = [jnp.exp(x - m0) for x in sv0]
            ex1 = [jnp.exp(x - m1) for x in sv1]
            z = jnp.zeros((SUBLANES, LANES), F32)
            cnt = [jnp.zeros((SUBLANES, LANES), F32)] * TOPK
            for (p, q), cnd in zip(pq, cands):
                sel = cnd >= thr
                z = z + jnp.where(sel, ex0[p] * ex1[q], 0.0)
                cnt[p] = cnt[p] + jnp.where(sel, 1.0, 0.0)
            for k in range(TOPK):
                sv0_ref[k, :, sl] = sv0[k]
                sv1_ref[k, :, sl] = sv1[k]
                lcnt_ref[k, :, sl] = cnt[k]
            m0_ref[:, sl] = m0
            m1_ref[:, sl] = m1
            zinv_ref[:, sl] = 1.0 / z

            for hd in range(heads):
                if xu_queue:
                    xu_queue.pop(0)()
                hrow = slice(hd, hd + 1)
                s0t = s0_ref[hd, :, sl]
                s1t = s1_ref[hd, :, sl]
                lf = jnp.broadcast_to(lcnt_ref[0, hrow, sl], s0t.shape)
                r1 = jnp.zeros(s1t.shape, F32)
                for k in range(TOPK):
                    nxt = lcnt_ref[k + 1, hrow, sl] if k + 1 < TOPK else 0.0
                    lf = jnp.where(sv0_ref[k, hrow, sl] > s0t, nxt, lf)
                    r1 = jnp.where(sv1_ref[k, hrow, sl] > s1t, float(k + 1), r1)
                e0 = jnp.exp(s0t - m0_ref[hrow, sl]) * zinv_ref[hrow, sl]
                e1 = jnp.exp(s1t - m1_ref[hrow, sl])
                r1b = r1.astype(BF16)
                e1b = e1.astype(BF16)
                for g in range(n_keys // PACK):
                    r1_ref[hd, g, :, sl] = r1b[g * PACK:(g + 1) * PACK, :]
                    e1_ref[hd, g, :, sl] = e1b[g * PACK:(g + 1) * PACK, :]
                lfp_ref[hd, c] = lf.reshape(n_keys // SUBLANES, SUBLANES, LANES)
                e0p_ref[hd, c] = e0.reshape(n_keys // SUBLANES, SUBLANES, LANES)
            while xu_queue:
                xu_queue.pop(0)()
            return carry

        assert i_per_step % n_tiles == 0
        lax.fori_loop(0, n_tiles, tile_body, 0)

    def bcast_row(ref, hd, lt, blk, i):
        parts = [jnp.broadcast_to(ref[hd, lt * (W // LANES) + k, blk][i:i + 1, :], (PACK, LANES)).astype(BF16)
                 for k in range(W // LANES)]
        return jnp.concatenate(parts, axis=-1)[None]

    zero_b = jnp.zeros((), BF16)

    def gate_tile(blk, st_ref, coef_ref, i, lt):
        ls = slice(lt * W, (lt + 1) * W)
        gsum = jnp.zeros((n_keys // PACK, PACK, W), BF16)
        for hd in range(heads):
            mask = r1_ref[hd, :, :, ls] < bcast_row(lfp_ref, hd, lt, blk, i)
            gsum = gsum + jnp.where(mask, e1_ref[hd, :, :, ls], zero_b) * bcast_row(e0p_ref, hd, lt, blk, i)
        hval = _gelu_tanh_bf16(st_ref[i, :, ls])
        for g in range(n_keys // PACK):
            lo = i * n_keys + g * PACK
            coef_ref[lo:lo + PACK, ls] = hval[g * PACK:(g + 1) * PACK, :] * gsum[g]

    def v_tile(half, coef_ref, lt):
        ls = slice(lt * W, (lt + 1) * W)
        accT_ref[:, ls] += _dot(vT_ref[:, half * E:(half + 1) * E], coef_ref[:, ls])

    def phase(xu=None, gate=None, vmm=None):
        for lt in range(n_lt):
            for i in range(i_per_step):
                if vmm is not None and i == 0:
                    v_tile(*vmm, lt)
                if xu is not None:
                    xu_tile(*xu, i, lt)
                if gate is not None:
                    gate_tile(*gate, i, lt)

    @pl.when(j == 0)
    def _first():
        first_step()
        phase(gate=(0, sa_ref, ca_ref))

    @pl.when(jnp.logical_and(j > 0, j < n_pairs))
    def _steady():
        phase(xu=(0, sa_ref), gate=(2 * j - 1, sb_ref, cb_ref), vmm=(0, ca_ref))
        phase(xu=(1, sb_ref), gate=(2 * j, sa_ref, ca_ref), vmm=(1, cb_ref))

    @pl.when(j == n_pairs)
    def _last():
        phase(gate=(2 * j - 1, sb_ref, cb_ref), vmm=(0, ca_ref))
        phase(vmm=(1, cb_ref))
        out_ref[...] = h_ref[...] + accT_ref[...].T


def _peer(h, gffn, wqT, keys, u_bf, vT_bf, *, T, E, W):
    N, D = h.shape
    heads = keys.shape[0] // 2
    n_keys = keys.shape[1]
    n_exp = u_bf.shape[0]
    n_pairs = n_exp // (2 * E)
    once = pl.Buffered(1)
    kern = functools.partial(_peer_kernel, T=T, E=E, W=W, heads=heads)
    return pl.pallas_call(
        kern,
        grid=(N // T, n_pairs + 1),
        in_specs=[
            pl.BlockSpec((T, D), lambda t, j: (t, 0)),
            pl.BlockSpec(gffn.shape, lambda t, j: (0, 0)),
            pl.BlockSpec(wqT.shape, lambda t, j: (0, 0), pipeline_mode=once),
            pl.BlockSpec(keys.shape, lambda t, j: (0, 0, 0), pipeline_mode=once),
            pl.BlockSpec((2 * E, D), lambda t, j: (jnp.minimum(j, n_pairs - 1), 0)),
            pl.BlockSpec((D, 2 * E), lambda t, j: (0, jnp.maximum(j - 1, 0))),
        ],
        out_specs=pl.BlockSpec((T, D), lambda t, j: (t, 0)),
        out_shape=jax.ShapeDtypeStruct((N, D), F32),
        scratch_shapes=[
            pltpu.VMEM((D, T), BF16),
            pltpu.VMEM((D, T), F32),
            pltpu.VMEM((E // n_keys, n_keys, T), F32),
            pltpu.VMEM((E // n_keys, n_keys, T), F32),
            pltpu.VMEM((E, T), BF16),
            pltpu.VMEM((E, T), BF16),
            pltpu.VMEM((heads, n_keys, T), F32),
            pltpu.VMEM((heads, n_keys, T), F32),
            pltpu.VMEM((heads, n_keys // PACK, PACK, T), BF16),
            pltpu.VMEM((heads, n_keys // PACK, PACK, T), BF16),
            pltpu.VMEM((heads, T // LANES, n_keys // SUBLANES, SUBLANES, LANES), F32),
            pltpu.VMEM((heads, T // LANES, n_keys // SUBLANES, SUBLANES, LANES), F32),
            pltpu.VMEM((TOPK, SUBLANES, T), F32),
            pltpu.VMEM((TOPK, SUBLANES, T), F32),
            pltpu.VMEM((TOPK, SUBLANES, T), F32),
            pltpu.VMEM((SUBLANES, T), F32),
            pltpu.VMEM((SUBLANES, T), F32),
            pltpu.VMEM((SUBLANES, T), F32),
        ],
        compiler_params=pltpu.CompilerParams(
            dimension_semantics=("arbitrary", "arbitrary"), vmem_limit_bytes=VMEM_LIMIT),
        name="peer",
    )(h, gffn, wqT, keys, u_bf, vT_bf)


def _ple_kernel(h_ref, p_ref, gple_ref, wgate_ref, wproj_ref, gfin_ref, out_ref, *, final):
    h = h_ref[...]
    hn = _rmsnorm(h, gple_ref[...]).astype(BF16)
    gate = _sigmoid(_dot(hn, wgate_ref[...]))
    proj = _dot(p_ref[...].astype(BF16), wproj_ref[...])
    h = h + proj * gate
    if final:
        h = _rmsnorm(h, gfin_ref[...])
    out_ref[...] = h


def _ple(h, p, gple, wgate, wproj, gfin, *, T, final):
    N, D = h.shape
    dp = p.shape[1]
    const2 = lambda t: (0, 0)
    return pl.pallas_call(
        functools.partial(_ple_kernel, final=final),
        grid=(N // T,),
        in_specs=[
            pl.BlockSpec((T, D), lambda t: (t, 0)),
            pl.BlockSpec((T, dp), lambda t: (t, 0)),
            pl.BlockSpec(gple.shape, const2),
            pl.BlockSpec(wgate.shape, const2),
            pl.BlockSpec(wproj.shape, const2),
            pl.BlockSpec(gfin.shape, const2),
        ],
        out_specs=pl.BlockSpec((T, D), lambda t: (t, 0)),
        out_shape=jax.ShapeDtypeStruct((N, D), F32),
        compiler_params=pltpu.CompilerParams(
            dimension_semantics=("arbitrary",), vmem_limit_bytes=VMEM_LIMIT),
        name="ple",
    )(h, p, gple, wgate, wproj, gfin)


def _block_diag(ws):
    g, n, _ = ws.shape
    out = jnp.zeros((g * n, g * n), ws.dtype)
    for k in range(g):
        out = out.at[k * n:(k + 1) * n, k * n:(k + 1) * n].set(ws[k])
    return out


def kernel(x_prompt, x_sample, state_pool, state_conv, p_prompt, p_sample, g_mix, w_in, w_pool, pool_scale, w_dw, b_dw, ln_g, ln_b, w_out, g_ffn, w_q, sub_keys, u_tab, v_tab, g_ple, w_ple_gate, w_ple_proj, g_final):
    depth = g_mix.shape[0]
    B, S, D = x_prompt.shape
    Bs, Ts, _ = x_sample.shape
    row = lambda v: v.reshape(1, -1)

    T_MIX = min(512, S)
    T_PEER = 512
    E_BLK = 1024
    W_GATE = 256
    BC = min(64, Bs)

    hp = x_prompt
    hs = jnp.transpose(x_sample, (1, 0, 2))
    pool_p, conv_p, pool_s, conv_s = [], [], [], []
    for i in range(depth):
        win = w_in[i].astype(BF16)
        wpool_bd = _block_diag(w_pool[i]).astype(BF16)
        wout = w_out[i].astype(BF16)
        mix_w = (row(g_mix[i]), win, wpool_bd, row(pool_scale[i]), w_dw[i], row(b_dw[i]), row(ln_g[i]),
                 row(ln_b[i]), wout)
        wqT = w_q[i].T.astype(BF16)
        heads, _, n_keys, d_half = sub_keys[i].shape
        keys = sub_keys[i].reshape(heads * 2, n_keys, d_half).astype(BF16)
        u_bf = u_tab[i].astype(BF16)
        vT_bf = v_tab[i].T.astype(BF16)
        wgate = w_ple_gate[i].astype(BF16)
        wproj = w_ple_proj[i].astype(BF16)
        final = i == depth - 1

        hp, npool, nconv = _mixer_prompt(hp, *mix_w, T=T_MIX)
        pool_p.append(npool[0])
        conv_p.append(nconv[0])
        hs, spool, sconv = _mixer_sample(
            hs, jnp.transpose(state_pool[i], (1, 0, 2)), jnp.transpose(state_conv[i], (1, 0, 2)), *mix_w, BC=BC)
        pool_s.append(jnp.transpose(spool, (1, 0, 2)))
        conv_s.append(jnp.transpose(sconv, (1, 0, 2)))

        hp2 = _peer(hp.reshape(B * S, D), row(g_ffn[i]), wqT, keys, u_bf, vT_bf, T=T_PEER, E=E_BLK, W=W_GATE)
        hs2 = _peer(hs.reshape(Ts * Bs, D), row(g_ffn[i]), wqT, keys, u_bf, vT_bf,
                    T=min(T_PEER, Ts * Bs), E=E_BLK, W=W_GATE)
        pp = p_prompt[i].reshape(B * S, -1)
        ps = jnp.transpose(p_sample[i], (1, 0, 2)).reshape(Ts * Bs, -1)
        hp = _ple(hp2, pp, row(g_ple[i]), wgate, wproj, row(g_final), T=T_PEER, final=final).reshape(B, S, D)
        hs = _ple(hs2, ps, row(g_ple[i]), wgate, wproj, row(g_final), T=min(T_PEER, Ts * Bs),
                  final=final).reshape(Ts, Bs, D)

    y_prompt = hp
    y_sample = jnp.transpose(hs, (1, 0, 2))
    return (y_prompt, y_sample, jnp.stack(pool_p), jnp.stack(conv_p), jnp.stack(pool_s), jnp.stack(conv_s))
```

```python
import functools

import jax
import jax.numpy as jnp
from jax import lax
from jax.experimental import pallas as pl
from jax.experimental.pallas import tpu as pltpu

EPS = 1e-6
PAST_LEN = 16384
POOL_WINDOWS = (2, 4, 8, 16)
TOPK = 16

LANES = 128
SUBLANES = 8
PACK = 16
VMEM_LIMIT = 56 * 1024 * 1024

BF16 = jnp.bfloat16
F32 = jnp.float32


def _dot(a, b):
    return jnp.dot(a, b, preferred_element_type=F32)


def _rmsnorm(x, g):
    return x * lax.rsqrt(jnp.mean(x * x, axis=-1, keepdims=True) + EPS) * g


def _sigmoid(x):
    return 1.0 / (1.0 + jnp.exp(-x))


def _gelu_tanh_bf16(x):
    c = 0.7978845608028654
    inner = (x * (x * x * (c * 0.044715) + c)).astype(BF16)
    hx = x.astype(BF16) * 0.5
    return hx * jnp.tanh(inner) + hx


def _conv_tail(y, bdw, lng, lnb):
    y = y + bdw
    mu = jnp.mean(y, axis=-1, keepdims=True)
    d = y - mu
    var = jnp.mean(d * d, axis=-1, keepdims=True)
    yn = d * lax.rsqrt(var + EPS) * lng + lnb
    return yn * _sigmoid(yn)


A_PAD = 16
U_PAD = 32


def _mixer_prompt_kernel(x_ref, gmix_ref, win_ref, wpool_ref, pscale_ref, wdw_ref, bdw_ref, lng_ref, lnb_ref,
                         wout_ref, h_ref, pool_out_ref, conv_out_ref, abuf, ubuf, *, T, d_pool, d_conv):
    s = pl.program_id(1)
    n_s = pl.num_programs(1)
    conv_width = wdw_ref.shape[0]
    pool_state = max(POOL_WINDOWS) - 1
    conv_state = conv_width - 1

    @pl.when(s == 0)
    def _():
        abuf[0:A_PAD, :] = jnp.zeros((A_PAD, d_pool), F32)
        ubuf[0:U_PAD, :] = jnp.zeros((U_PAD, d_conv), F32)
        ubuf[U_PAD + T:U_PAD + T + SUBLANES, :] = jnp.zeros((SUBLANES, d_conv), F32)

    x = x_ref[0]
    xn = _rmsnorm(x, gmix_ref[...]).astype(BF16)
    z = _dot(xn, win_ref[...])
    a = z[:, :d_pool]
    val = z[:, d_pool:d_pool + d_conv]
    gt = z[:, d_pool + d_conv:]
    u = val * _sigmoid(gt)
    abuf[A_PAD:A_PAD + T, :] = a
    ubuf[U_PAD:U_PAD + T, :] = u

    group = d_pool // len(POOL_WINDOWS)
    pos = s * T + lax.broadcasted_iota(jnp.int32, (T, group), 0)
    ys = []
    for g, w in enumerate(POOL_WINDOWS):
        lo = g * group
        win = abuf[A_PAD:A_PAD + T, lo:lo + group]
        for jj in range(1, w):
            win = win + abuf[A_PAD - jj:A_PAD - jj + T, lo:lo + group]
        cnt = jnp.minimum(pos + 1, w).astype(F32)
        ys.append(win / cnt - a[:, lo:lo + group])
    ypool = jnp.concatenate(ys, axis=-1).astype(BF16)
    ya = _dot(ypool, wpool_ref[...]) * pscale_ref[...]

    off = U_PAD - conv_state
    y = None
    for b in range(SUBLANES):
        q = None
        for a in range((off + conv_width - 1) // SUBLANES + 1):
            k = a * SUBLANES + b - off
            if 0 <= k < conv_width:
                term = ubuf[a * SUBLANES:a * SUBLANES + T + SUBLANES, :] * wdw_ref[k:k + 1, :]
                q = term if q is None else q + term
        part = q[b:b + T, :]
        y = part if y is None else y + part
    yb = _conv_tail(y, bdw_ref[...], lng_ref[...], lnb_ref[...])

    cat = jnp.concatenate([ya, yb], axis=-1).astype(BF16)
    h_ref[0] = x + _dot(cat, wout_ref[...])

    @pl.when(s == n_s - 1)
    def _():
        pool_out_ref[0, 0] = abuf[A_PAD + T - pool_state:A_PAD + T, :]
        conv_out_ref[0, 0] = ubuf[U_PAD + T - conv_state:U_PAD + T, :]

    abuf[0:A_PAD, :] = abuf[T:T + A_PAD, :]
    ubuf[0:U_PAD, :] = ubuf[T:T + U_PAD, :]


def _mixer_prompt(x, gmix, win, wpool_bd, pscale, wdw, bdw, lng, lnb, wout, *, T):
    B, S, D = x.shape
    d_pool = wpool_bd.shape[0]
    d_conv = wdw.shape[1]
    pool_state = max(POOL_WINDOWS) - 1
    conv_state = wdw.shape[0] - 1
    const2 = lambda b, s: (0, 0)
    kern = functools.partial(_mixer_prompt_kernel, T=T, d_pool=d_pool, d_conv=d_conv)
    return pl.pallas_call(
        kern,
        grid=(B, S // T),
        in_specs=[
            pl.BlockSpec((1, T, D), lambda b, s: (b, s, 0)),
            pl.BlockSpec(gmix.shape, const2),
            pl.BlockSpec(win.shape, const2),
            pl.BlockSpec(wpool_bd.shape, const2),
            pl.BlockSpec(pscale.shape, const2),
            pl.BlockSpec(wdw.shape, const2),
            pl.BlockSpec(bdw.shape, const2),
            pl.BlockSpec(lng.shape, const2),
            pl.BlockSpec(lnb.shape, const2),
            pl.BlockSpec(wout.shape, const2),
        ],
        out_specs=[
            pl.BlockSpec((1, T, D), lambda b, s: (b, s, 0)),
            pl.BlockSpec((1, 1, pool_state, d_pool), lambda b, s: (0, b, 0, 0)),
            pl.BlockSpec((1, 1, conv_state, d_conv), lambda b, s: (0, b, 0, 0)),
        ],
        out_shape=[
            jax.ShapeDtypeStruct((B, S, D), F32),
            jax.ShapeDtypeStruct((1, B, pool_state, d_pool), F32),
            jax.ShapeDtypeStruct((1, B, conv_state, d_conv), F32),
        ],
        scratch_shapes=[
            pltpu.VMEM((T + A_PAD, d_pool), F32),
            pltpu.VMEM((T + U_PAD + SUBLANES, d_conv), F32),
        ],
        compiler_params=pltpu.CompilerParams(
            dimension_semantics=("arbitrary", "arbitrary"), vmem_limit_bytes=VMEM_LIMIT),
        name="mixer_prompt",
    )(x, gmix, win, wpool_bd, pscale, wdw, bdw, lng, lnb, wout)


def _mixer_sample_kernel(x_ref, pool_ref, conv_ref, gmix_ref, win_ref, wpool_ref, pscale_ref, wdw_ref, bdw_ref,
                         lng_ref, lnb_ref, wout_ref, h_ref, pool_out_ref, conv_out_ref, *, d_pool, d_conv):
    Ts, BC, D = x_ref.shape
    P = pool_ref.shape[0]
    C = conv_ref.shape[0]
    conv_width = wdw_ref.shape[0]

    x = x_ref[...].reshape(Ts * BC, D)
    xn = _rmsnorm(x, gmix_ref[...]).astype(BF16)
    z = _dot(xn, win_ref[...])
    a = z[:, :d_pool]
    u = z[:, d_pool:d_pool + d_conv] * _sigmoid(z[:, d_pool + d_conv:])

    def full_a(k):
        return pool_ref[k] if k < P else a[(k - P) * BC:(k - P + 1) * BC, :]

    def full_u(k):
        return conv_ref[k] if k < C else u[(k - C) * BC:(k - C + 1) * BC, :]

    group = d_pool // len(POOL_WINDOWS)
    ya_rows, yb_rows = [], []
    for t in range(Ts):
        ys = []
        for g, w in enumerate(POOL_WINDOWS):
            lo = g * group
            win = full_a(P + t)[:, lo:lo + group]
            for jj in range(1, w):
                win = win + full_a(P + t - jj)[:, lo:lo + group]
            cnt = float(min(PAST_LEN + t + 1, w))
            ys.append(win / cnt - full_a(P + t)[:, lo:lo + group])
        ya_rows.append(jnp.concatenate(ys, axis=-1))
        y = full_u(t) * wdw_ref[0:1, :]
        for k in range(1, conv_width):
            y = y + full_u(t + k) * wdw_ref[k:k + 1, :]
        yb_rows.append(y)
    ypool = jnp.concatenate(ya_rows, axis=0).astype(BF16)
    ya = _dot(ypool, wpool_ref[...]) * pscale_ref[...]
    yb = _conv_tail(jnp.concatenate(yb_rows, axis=0), bdw_ref[...], lng_ref[...], lnb_ref[...])
    cat = jnp.concatenate([ya, yb], axis=-1).astype(BF16)
    h = x + _dot(cat, wout_ref[...])
    h_ref[...] = h.reshape(Ts, BC, D)
    for k in range(P):
        pool_out_ref[k] = full_a(Ts + k)
    for k in range(C):
        conv_out_ref[k] = full_u(Ts + k)


def _mixer_sample(x_tm, pool_tm, conv_tm, gmix, win, wpool_bd, pscale, wdw, bdw, lng, lnb, wout, *, BC):
    Ts, B, D = x_tm.shape
    P, _, d_pool = pool_tm.shape
    C, _, d_conv = conv_tm.shape
    const2 = lambda b: (0, 0)
    kern = functools.partial(_mixer_sample_kernel, d_pool=d_pool, d_conv=d_conv)
    return pl.pallas_call(
        kern,
        grid=(B // BC,),
        in_specs=[
            pl.BlockSpec((Ts, BC, D), lambda b: (0, b, 0)),
            pl.BlockSpec((P, BC, d_pool), lambda b: (0, b, 0)),
            pl.BlockSpec((C, BC, d_conv), lambda b: (0, b, 0)),
            pl.BlockSpec(gmix.shape, const2),
            pl.BlockSpec(win.shape, const2),
            pl.BlockSpec(wpool_bd.shape, const2),
            pl.BlockSpec(pscale.shape, const2),
            pl.BlockSpec(wdw.shape, const2),
            pl.BlockSpec(bdw.shape, const2),
            pl.BlockSpec(lng.shape, const2),
            pl.BlockSpec(lnb.shape, const2),
            pl.BlockSpec(wout.shape, const2),
        ],
        out_specs=[
            pl.BlockSpec((Ts, BC, D), lambda b: (0, b, 0)),
            pl.BlockSpec((P, BC, d_pool), lambda b: (0, b, 0)),
            pl.BlockSpec((C, BC, d_conv), lambda b: (0, b, 0)),
        ],
        out_shape=[
            jax.ShapeDtypeStruct((Ts, B, D), F32),
            jax.ShapeDtypeStruct((P, B, d_pool), F32),
            jax.ShapeDtypeStruct((C, B, d_conv), F32),
        ],
        compiler_params=pltpu.CompilerParams(
            dimension_semantics=("arbitrary",), vmem_limit_bytes=VMEM_LIMIT),
        name="mixer_sample",
    )(x_tm, pool_tm, conv_tm, gmix, win, wpool_bd, pscale, wdw, bdw, lng, lnb, wout)


def _batcher_pairs(n):
    pairs = []
    t = (n - 1).bit_length()
    p = 1 << (t - 1)
    while p > 0:
        q = 1 << (t - 1)
        r = 0
        d = p
        while d > 0:
            for i in range(n - d):
                if (i & p) == r:
                    pairs.append((i, i + d))
            d = q - p
            q >>= 1
            r = p
        p >>= 1
    return pairs


_SORT16 = _batcher_pairs(TOPK)


def _top16_sorted(s):
    n = s.shape[0] // SUBLANES
    v = [s[k * SUBLANES:(k + 1) * SUBLANES, :] for k in range(n)]
    for i, j in _SORT16:
        hi = jnp.maximum(v[i], v[j])
        lo = jnp.minimum(v[i], v[j])
        v[i], v[j] = hi, lo
    for sh in (4, 2, 1):
        b = [pltpu.roll(x, sh, 0) for x in v]
        v = [jnp.maximum(v[k], b[n - 1 - k]) for k in range(n)]
        st = n // 2
        while st >= 1:
            for k in range(n):
                if not (k & st):
                    hi = jnp.maximum(v[k], v[k + st])
                    lo = jnp.minimum(v[k], v[k + st])
                    v[k], v[k + st] = hi, lo
            st //= 2
    return v


def _kth_largest(cands, k):
    def tree_max(xs):
        xs = list(xs)
        while len(xs) > 1:
            nxt = [jnp.maximum(xs[i], xs[i + 1]) for i in range(0, len(xs) - 1, 2)]
            if len(xs) % 2:
                nxt.append(xs[-1])
            xs = nxt
        return xs[0]

    cur = list(cands)
    for _ in range(k - 1):
        m = tree_max(cur)
        cur = [jnp.where(c == m, -jnp.inf, c) for c in cur]
    return tree_max(cur)


def _peer_kernel(h_ref, gffn_ref, wqT_ref, keys_ref, u_ref, vT_ref, p_ref, gple_ref, wgate_ref, wproj_ref, gfin_ref,
                 out_ref,
                 xnT_ref, accT_ref, sa_ref, sb_ref, ca_ref, cb_ref, s0_ref, s1_ref, r1_ref, e1_ref, lfp_ref, e0p_ref,
                 sv0_ref, sv1_ref, lcnt_ref, m0_ref, m1_ref, zinv_ref,
                 *, T, E, W, heads, final):
    j = pl.program_id(1)
    n_pairs = pl.num_programs(1) - 1
    n_keys = keys_ref.shape[1]
    d_half = keys_ref.shape[2]
    n_tiles = T // LANES
    i_per_step = E // n_keys
    n_lt = T // W
    assert i_per_step == SUBLANES and n_keys == 8 * PACK

    def xu_tile(half, st_ref, i, lt):
        ls = slice(lt * W, (lt + 1) * W)
        lo = half * E + i * n_keys
        rows = slice(lo, lo + n_keys) if isinstance(lo, int) else pl.ds(pl.multiple_of(lo, n_keys), n_keys)
        st_ref[i, :, ls] = _dot(u_ref[rows, :], xnT_ref[:, ls])

    def first_step():
        xn = _rmsnorm(h_ref[...], gffn_ref[...])
        xnT = xn.T.astype(BF16)
        xnT_ref[...] = xnT
        accT_ref[...] = jnp.zeros(accT_ref.shape, F32)
        for hd in range(heads):
            qT = _dot(wqT_ref[hd * 2 * d_half:(hd + 1) * 2 * d_half, :], xnT).astype(BF16)
            s0_ref[hd] = _dot(keys_ref[2 * hd], qT[:d_half, :])
            s1_ref[hd] = _dot(keys_ref[2 * hd + 1], qT[d_half:, :])

        sub = lax.broadcasted_iota(jnp.int32, (SUBLANES, LANES), 0)
        pq = [(p, q) for p in range(TOPK) for q in range(TOPK) if (p + 1) * (q + 1) <= TOPK]
        xu_per_tile = i_per_step // n_tiles

        def tile_body(c, carry):
            sl = pl.ds(pl.multiple_of(c * LANES, LANES), LANES)
            xu_queue = [functools.partial(xu_tile, half, st, c * xu_per_tile + k, lt)
                        for half, st in ((0, sa_ref), (1, sb_ref)) for lt in range(n_lt) for k in range(xu_per_tile)]
            sv0 = [jnp.zeros((SUBLANES, LANES), F32)] * TOPK
            sv1 = [jnp.zeros((SUBLANES, LANES), F32)] * TOPK
            for hd in range(heads):
                if xu_queue:
                    xu_queue.pop(0)()
                t0 = _top16_sorted(s0_ref[hd, :, sl])
                t1 = _top16_sorted(s1_ref[hd, :, sl])
                sv0 = [jnp.where(sub == hd, t0[k], sv0[k]) for k in range(TOPK)]
                sv1 = [jnp.where(sub == hd, t1[k], sv1[k]) for k in range(TOPK)]
            cands = [sv0[p] + sv1[q] for p, q in pq]
            thr = _kth_largest(cands, TOPK)
            m0 = sv0[0]
            m1 = sv1[0]
            ex0 = [jnp.exp(x - m0) for x in sv0]
            ex1 = [jnp.exp(x - m1) for x in sv1]
            z = jnp.zeros((SUBLANES, LANES), F32)
            cnt = [jnp.zeros((SUBLANES, LANES), F32)] * TOPK
            for (p, q), cnd in zip(pq, cands):
                sel = cnd >= thr
                z = z + jnp.where(sel, ex0[p] * ex1[q], 0.0)
                cnt[p] = cnt[p] + jnp.where(sel, 1.0, 0.0)
            for k in range(TOPK):
                sv0_ref[k, :, sl] = sv0[k]
                sv1_ref[k, :, sl] = sv1[k]
                lcnt_ref[k, :, sl] = cnt[k]
            m0_ref[:, sl] = m0
            m1_ref[:, sl] = m1
            zinv_ref[:, sl] = 1.0 / z

            for hd in range(heads):
                if xu_queue:
                    xu_queue.pop(0)()
                hrow = slice(hd, hd + 1)
                s0t = s0_ref[hd, :, sl]
                s1t = s1_ref[hd, :, sl]
                lf = jnp.broadcast_to(lcnt_ref[0, hrow, sl], s0t.shape)
                r1 = jnp.zeros(s1t.shape, F32)
                for k in range(TOPK):
                    nxt = lcnt_ref[k + 1, hrow, sl] if k + 1 < TOPK else 0.0
                    lf = jnp.where(sv0_ref[k, hrow, sl] > s0t, nxt, lf)
                    r1 = jnp.where(sv1_ref[k, hrow, sl] > s1t, float(k + 1), r1)
                e0 = jnp.exp(s0t - m0_ref[hrow, sl]) * zinv_ref[hrow, sl]
                e1 = jnp.exp(s1t - m1_ref[hrow, sl])
                r1b = r1.astype(BF16)
                e1b = e1.astype(BF16)
                for g in range(n_keys // PACK):
                    r1_ref[hd, g, :, sl] = r1b[g * PACK:(g + 1) * PACK, :]
                    e1_ref[hd, g, :, sl] = e1b[g * PACK:(g + 1) * PACK, :]
                lfp_ref[hd, c] = lf.reshape(n_keys // SUBLANES, SUBLANES, LANES)
                e0p_ref[hd, c] = e0.reshape(n_keys // SUBLANES, SUBLANES, LANES)
            while xu_queue:
                xu_queue.pop(0)()
            return carry

        assert i_per_step % n_tiles == 0
        lax.fori_loop(0, n_tiles, tile_body, 0)

    def bcast_row(ref, hd, lt, blk, i):
        parts = [jnp.broadcast_to(ref[hd, lt * (W // LANES) + k, blk][i:i + 1, :], (PACK, LANES)).astype(BF16)
                 for k in range(W // LANES)]
        return jnp.concatenate(parts, axis=-1)[None]

    zero_b = jnp.zeros((), BF16)

    def gate_tile(blk, st_ref, coef_ref, i, lt):
        ls = slice(lt * W, (lt + 1) * W)
        gsum = jnp.zeros((n_keys // PACK, PACK, W), BF16)
        for hd in range(heads):
            mask = r1_ref[hd, :, :, ls] < bcast_row(lfp_ref, hd, lt, blk, i)
            gsum = gsum + jnp.where(mask, e1_ref[hd, :, :, ls], zero_b) * bcast_row(e0p_ref, hd, lt, blk, i)
        hval = _gelu_tanh_bf16(st_ref[i, :, ls])
        for g in range(n_keys // PACK):
            lo = i * n_keys + g * PACK
            coef_ref[lo:lo + PACK, ls] = hval[g * PACK:(g + 1) * PACK, :] * gsum[g]

    def v_tile(half, coef_ref, lt):
        ls = slice(lt * W, (lt + 1) * W)
        accT_ref[:, ls] += _dot(vT_ref[:, half * E:(half + 1) * E], coef_ref[:, ls])

    def phase(xu=None, gate=None, vmm=None):
        for lt in range(n_lt):
            for i in range(i_per_step):
                if vmm is not None and i == 0:
                    v_tile(*vmm, lt)
                if xu is not None:
                    xu_tile(*xu, i, lt)
                if gate is not None:
                    gate_tile(*gate, i, lt)

    @pl.when(j == 0)
    def _first():
        first_step()
        phase(gate=(0, sa_ref, ca_ref))

    @pl.when(jnp.logical_and(j > 0, j < n_pairs))
    def _steady():
        phase(xu=(0, sa_ref), gate=(2 * j - 1, sb_ref, cb_ref), vmm=(0, ca_ref))
        phase(xu=(1, sb_ref), gate=(2 * j, sa_ref, ca_ref), vmm=(1, cb_ref))

    @pl.when(j == n_pairs)
    def _last():
        phase(gate=(2 * j - 1, sb_ref, cb_ref), vmm=(0, ca_ref))
        proj = _dot(p_ref[...].astype(BF16), wproj_ref[...])
        phase(vmm=(1, cb_ref))
        h = h_ref[...] + accT_ref[...].T
        hn = _rmsnorm(h, gple_ref[...]).astype(BF16)
        gate = _sigmoid(_dot(hn, wgate_ref[...]))
        h = h + proj * gate
        if final:
            h = _rmsnorm(h, gfin_ref[...])
        out_ref[...] = h


def _peer(h, gffn, wqT, keys, u_bf, vT_bf, p, gple, wgate, wproj, gfin, *, T, E, W, final):
    N, D = h.shape
    heads = keys.shape[0] // 2
    n_keys = keys.shape[1]
    n_exp = u_bf.shape[0]
    n_pairs = n_exp // (2 * E)
    once = pl.Buffered(1)
    const2 = lambda t, j: (0, 0)
    kern = functools.partial(_peer_kernel, T=T, E=E, W=W, heads=heads, final=final)
    return pl.pallas_call(
        kern,
        grid=(N // T, n_pairs + 1),
        in_specs=[
            pl.BlockSpec((T, D), lambda t, j: (t, 0)),
            pl.BlockSpec(gffn.shape, const2),
            pl.BlockSpec(wqT.shape, const2, pipeline_mode=once),
            pl.BlockSpec(keys.shape, lambda t, j: (0, 0, 0), pipeline_mode=once),
            pl.BlockSpec((2 * E, D), lambda t, j: (jnp.minimum(j, n_pairs - 1), 0)),
            pl.BlockSpec((D, 2 * E), lambda t, j: (0, jnp.maximum(j - 1, 0))),
            pl.BlockSpec((T, p.shape[1]), lambda t, j: (t, 0)),
            pl.BlockSpec(gple.shape, const2),
            pl.BlockSpec(wgate.shape, const2, pipeline_mode=once),
            pl.BlockSpec(wproj.shape, const2, pipeline_mode=once),
            pl.BlockSpec(gfin.shape, const2),
        ],
        out_specs=pl.BlockSpec((T, D), lambda t, j: (t, 0)),
        out_shape=jax.ShapeDtypeStruct((N, D), F32),
        scratch_shapes=[
            pltpu.VMEM((D, T), BF16),
            pltpu.VMEM((D, T), F32),
            pltpu.VMEM((E // n_keys, n_keys, T), F32),
            pltpu.VMEM((E // n_keys, n_keys, T), F32),
            pltpu.VMEM((E, T), BF16),
            pltpu.VMEM((E, T), BF16),
            pltpu.VMEM((heads, n_keys, T), F32),
            pltpu.VMEM((heads, n_keys, T), F32),
            pltpu.VMEM((heads, n_keys // PACK, PACK, T), BF16),
            pltpu.VMEM((heads, n_keys // PACK, PACK, T), BF16),
            pltpu.VMEM((heads, T // LANES, n_keys // SUBLANES, SUBLANES, LANES), F32),
            pltpu.VMEM((heads, T // LANES, n_keys // SUBLANES, SUBLANES, LANES), F32),
            pltpu.VMEM((TOPK, SUBLANES, T), F32),
            pltpu.VMEM((TOPK, SUBLANES, T), F32),
            pltpu.VMEM((TOPK, SUBLANES, T), F32),
            pltpu.VMEM((SUBLANES, T), F32),
            pltpu.VMEM((SUBLANES, T), F32),
            pltpu.VMEM((SUBLANES, T), F32),
        ],
        compiler_params=pltpu.CompilerParams(
            dimension_semantics=("arbitrary", "arbitrary"), vmem_limit_bytes=VMEM_LIMIT),
        name="peer",
    )(h, gffn, wqT, keys, u_bf, vT_bf, p, gple, wgate, wproj, gfin)


def _block_diag(ws):
    g, n, _ = ws.shape
    out = jnp.zeros((g * n, g * n), ws.dtype)
    for k in range(g):
        out = out.at[k * n:(k + 1) * n, k * n:(k + 1) * n].set(ws[k])
    return out


def kernel(x_prompt, x_sample, state_pool, state_conv, p_prompt, p_sample, g_mix, w_in, w_pool, pool_scale, w_dw, b_dw, ln_g, ln_b, w_out, g_ffn, w_q, sub_keys, u_tab, v_tab, g_ple, w_ple_gate, w_ple_proj, g_final):
    depth = g_mix.shape[0]
    B, S, D = x_prompt.shape
    Bs, Ts, _ = x_sample.shape
    row = lambda v: v.reshape(1, -1)

    T_MIX = min(512, S)
    T_PEER = 512
    E_BLK = 1024
    W_GATE = 256
    BC = min(64, Bs)

    hp = x_prompt
    hs = jnp.transpose(x_sample, (1, 0, 2))
    pool_p, conv_p, pool_s, conv_s = [], [], [], []
    for i in range(depth):
        win = w_in[i].astype(BF16)
        wpool_bd = _block_diag(w_pool[i]).astype(BF16)
        wout = w_out[i].astype(BF16)
        mix_w = (row(g_mix[i]), win, wpool_bd, row(pool_scale[i]), w_dw[i], row(b_dw[i]), row(ln_g[i]),
                 row(ln_b[i]), wout)
        wqT = w_q[i].T.astype(BF16)
        heads, _, n_keys, d_half = sub_keys[i].shape
        keys = sub_keys[i].reshape(heads * 2, n_keys, d_half).astype(BF16)
        u_bf = u_tab[i].astype(BF16)
        vT_bf = v_tab[i].T.astype(BF16)
        wgate = w_ple_gate[i].astype(BF16)
        wproj = w_ple_proj[i].astype(BF16)
        final = i == depth - 1

        hp, npool, nconv = _mixer_prompt(hp, *mix_w, T=T_MIX)
        pool_p.append(npool[0])
        conv_p.append(nconv[0])
        hs, spool, sconv = _mixer_sample(
            hs, jnp.transpose(state_pool[i], (1, 0, 2)), jnp.transpose(state_conv[i], (1, 0, 2)), *mix_w, BC=BC)
        pool_s.append(jnp.transpose(spool, (1, 0, 2)))
        conv_s.append(jnp.transpose(sconv, (1, 0, 2)))

        pp = p_prompt[i].reshape(B * S, -1)
        ps = jnp.transpose(p_sample[i], (1, 0, 2)).reshape(Ts * Bs, -1)
        peer_w = (row(g_ffn[i]), wqT, keys, u_bf, vT_bf)
        ple_w = (row(g_ple[i]), wgate, wproj, row(g_final))
        hp = _peer(hp.reshape(B * S, D), *peer_w, pp, *ple_w,
                   T=T_PEER, E=E_BLK, W=W_GATE, final=final).reshape(B, S, D)
        hs = _peer(hs.reshape(Ts * Bs, D), *peer_w, ps, *ple_w,
                   T=min(T_PEER, Ts * Bs), E=E_BLK, W=W_GATE, final=final).reshape(Ts, Bs, D)

    y_prompt = hp
    y_sample = jnp.transpose(hs, (1, 0, 2))
    return (y_prompt, y_sample, jnp.stack(pool_p), jnp.stack(conv_p), jnp.stack(pool_s), jnp.stack(conv_s))
```

```python
import functools

import jax
import jax.numpy as jnp
from jax import lax
from jax.experimental import pallas as pl
from jax.experimental.pallas import tpu as pltpu

EPS = 1e-6
PAST_LEN = 16384
POOL_WINDOWS = (2, 4, 8, 16)
TOPK = 16

LANES = 128
SUBLANES = 8
PACK = 16
VMEM_LIMIT = 56 * 1024 * 1024

BF16 = jnp.bfloat16
F32 = jnp.float32


def _dot(a, b):
    return jnp.dot(a, b, preferred_element_type=F32)


def _rmsnorm(x, g):
    return x * lax.rsqrt(jnp.mean(x * x, axis=-1, keepdims=True) + EPS) * g


def _sigmoid(x):
    return 1.0 / (1.0 + jnp.exp(-x))


def _gelu_tanh_bf16(x):
    c = 0.7978845608028654
    inner = (x * (x * x * (c * 0.044715) + c)).astype(BF16)
    hx = x.astype(BF16) * 0.5
    return hx * jnp.tanh(inner) + hx


def _conv_tail(y, bdw, lng, lnb):
    y = y + bdw
    mu = jnp.mean(y, axis=-1, keepdims=True)
    d = y - mu
    var = jnp.mean(d * d, axis=-1, keepdims=True)
    yn = d * lax.rsqrt(var + EPS) * lng + lnb
    return yn * _sigmoid(yn)


A_PAD = 16
U_PAD = 32


def _mixer_prompt_kernel(x_ref, gmix_ref, win_ref, wpool_ref, pscale_ref, wdw_ref, bdw_ref, lng_ref, lnb_ref,
                         wout_ref, h_ref, pool_out_ref, conv_out_ref, abuf, ubuf, *, T, d_pool, d_conv):
    s = pl.program_id(1)
    n_s = pl.num_programs(1)
    conv_width = wdw_ref.shape[0]
    pool_state = max(POOL_WINDOWS) - 1
    conv_state = conv_width - 1

    @pl.when(s == 0)
    def _():
        abuf[0:A_PAD, :] = jnp.zeros((A_PAD, d_pool), F32)
        ubuf[0:U_PAD, :] = jnp.zeros((U_PAD, d_conv), F32)
        ubuf[U_PAD + T:U_PAD + T + SUBLANES, :] = jnp.zeros((SUBLANES, d_conv), F32)

    x = x_ref[0]
    xn = _rmsnorm(x, gmix_ref[...]).astype(BF16)
    z = _dot(xn, win_ref[...])
    a = z[:, :d_pool]
    val = z[:, d_pool:d_pool + d_conv]
    gt = z[:, d_pool + d_conv:]
    u = val * _sigmoid(gt)
    abuf[A_PAD:A_PAD + T, :] = a
    ubuf[U_PAD:U_PAD + T, :] = u

    group = d_pool // len(POOL_WINDOWS)
    pos = s * T + lax.broadcasted_iota(jnp.int32, (T, group), 0)
    ys = []
    for g, w in enumerate(POOL_WINDOWS):
        lo = g * group
        win = abuf[A_PAD:A_PAD + T, lo:lo + group]
        for jj in range(1, w):
            win = win + abuf[A_PAD - jj:A_PAD - jj + T, lo:lo + group]
        cnt = jnp.minimum(pos + 1, w).astype(F32)
        ys.append(win / cnt - a[:, lo:lo + group])
    ypool = jnp.concatenate(ys, axis=-1).astype(BF16)
    ya = _dot(ypool, wpool_ref[...]) * pscale_ref[...]

    off = U_PAD - conv_state
    y = None
    for b in range(SUBLANES):
        q = None
        for a in range((off + conv_width - 1) // SUBLANES + 1):
            k = a * SUBLANES + b - off
            if 0 <= k < conv_width:
                term = ubuf[a * SUBLANES:a * SUBLANES + T + SUBLANES, :] * wdw_ref[k:k + 1, :]
                q = term if q is None else q + term
        part = q[b:b + T, :]
        y = part if y is None else y + part
    yb = _conv_tail(y, bdw_ref[...], lng_ref[...], lnb_ref[...])

    cat = jnp.concatenate([ya, yb], axis=-1).astype(BF16)
    h_ref[0] = x + _dot(cat, wout_ref[...])

    @pl.when(s == n_s - 1)
    def _():
        pool_out_ref[0, 0] = abuf[A_PAD + T - pool_state:A_PAD + T, :]
        conv_out_ref[0, 0] = ubuf[U_PAD + T - conv_state:U_PAD + T, :]

    abuf[0:A_PAD, :] = abuf[T:T + A_PAD, :]
    ubuf[0:U_PAD, :] = ubuf[T:T + U_PAD, :]


def _mixer_prompt(x, gmix, win, wpool_bd, pscale, wdw, bdw, lng, lnb, wout, *, T):
    B, S, D = x.shape
    d_pool = wpool_bd.shape[0]
    d_conv = wdw.shape[1]
    pool_state = max(POOL_WINDOWS) - 1
    conv_state = wdw.shape[0] - 1
    const2 = lambda b, s: (0, 0)
    kern = functools.partial(_mixer_prompt_kernel, T=T, d_pool=d_pool, d_conv=d_conv)
    return pl.pallas_call(
        kern,
        grid=(B, S // T),
        in_specs=[
            pl.BlockSpec((1, T, D), lambda b, s: (b, s, 0)),
            pl.BlockSpec(gmix.shape, const2),
            pl.BlockSpec(win.shape, const2),
            pl.BlockSpec(wpool_bd.shape, const2),
            pl.BlockSpec(pscale.shape, const2),
            pl.BlockSpec(wdw.shape, const2),
            pl.BlockSpec(bdw.shape, const2),
            pl.BlockSpec(lng.shape, const2),
            pl.BlockSpec(lnb.shape, const2),
            pl.BlockSpec(wout.shape, const2),
        ],
        out_specs=[
            pl.BlockSpec((1, T, D), lambda b, s: (b, s, 0)),
            pl.BlockSpec((1, 1, pool_state, d_pool), lambda b, s: (0, b, 0, 0)),
            pl.BlockSpec((1, 1, conv_state, d_conv), lambda b, s: (0, b, 0, 0)),
        ],
        out_shape=[
            jax.ShapeDtypeStruct((B, S, D), F32),
            jax.ShapeDtypeStruct((1, B, pool_state, d_pool), F32),
            jax.ShapeDtypeStruct((1, B, conv_state, d_conv), F32),
        ],
        scratch_shapes=[
            pltpu.VMEM((T + A_PAD, d_pool), F32),
            pltpu.VMEM((T + U_PAD + SUBLANES, d_conv), F32),
        ],
        compiler_params=pltpu.CompilerParams(
            dimension_semantics=("arbitrary", "arbitrary"), vmem_limit_bytes=VMEM_LIMIT),
        name="mixer_prompt",
    )(x, gmix, win, wpool_bd, pscale, wdw, bdw, lng, lnb, wout)


def _mixer_sample_kernel(x_ref, pool_ref, conv_ref, gmix_ref, win_ref, wpool_ref, pscale_ref, wdw_ref, bdw_ref,
                         lng_ref, lnb_ref, wout_ref, h_ref, pool_out_ref, conv_out_ref, *, d_pool, d_conv):
    Ts, BC, D = x_ref.shape
    P = pool_ref.shape[0]
    C = conv_ref.shape[0]
    conv_width = wdw_ref.shape[0]

    x = x_ref[...].reshape(Ts * BC, D)
    xn = _rmsnorm(x, gmix_ref[...]).astype(BF16)
    z = _dot(xn, win_ref[...])
    a = z[:, :d_pool]
    u = z[:, d_pool:d_pool + d_conv] * _sigmoid(z[:, d_pool + d_conv:])

    def full_a(k):
        return pool_ref[k] if k < P else a[(k - P) * BC:(k - P + 1) * BC, :]

    def full_u(k):
        return conv_ref[k] if k < C else u[(k - C) * BC:(k - C + 1) * BC, :]

    group = d_pool // len(POOL_WINDOWS)
    ya_rows, yb_rows = [], []
    for t in range(Ts):
        ys = []
        for g, w in enumerate(POOL_WINDOWS):
            lo = g * group
            win = full_a(P + t)[:, lo:lo + group]
            for jj in range(1, w):
                win = win + full_a(P + t - jj)[:, lo:lo + group]
            cnt = float(min(PAST_LEN + t + 1, w))
            ys.append(win / cnt - full_a(P + t)[:, lo:lo + group])
        ya_rows.append(jnp.concatenate(ys, axis=-1))
        y = full_u(t) * wdw_ref[0:1, :]
        for k in range(1, conv_width):
            y = y + full_u(t + k) * wdw_ref[k:k + 1, :]
        yb_rows.append(y)
    ypool = jnp.concatenate(ya_rows, axis=0).astype(BF16)
    ya = _dot(ypool, wpool_ref[...]) * pscale_ref[...]
    yb = _conv_tail(jnp.concatenate(yb_rows, axis=0), bdw_ref[...], lng_ref[...], lnb_ref[...])
    cat = jnp.concatenate([ya, yb], axis=-1).astype(BF16)
    h = x + _dot(cat, wout_ref[...])
    h_ref[...] = h.reshape(Ts, BC, D)
    for k in range(P):
        pool_out_ref[k] = full_a(Ts + k)
    for k in range(C):
        conv_out_ref[k] = full_u(Ts + k)


def _mixer_sample(x_tm, pool_tm, conv_tm, gmix, win, wpool_bd, pscale, wdw, bdw, lng, lnb, wout, *, BC):
    Ts, B, D = x_tm.shape
    P, _, d_pool = pool_tm.shape
    C, _, d_conv = conv_tm.shape
    const2 = lambda b: (0, 0)
    kern = functools.partial(_mixer_sample_kernel, d_pool=d_pool, d_conv=d_conv)
    return pl.pallas_call(
        kern,
        grid=(B // BC,),
        in_specs=[
            pl.BlockSpec((Ts, BC, D), lambda b: (0, b, 0)),
            pl.BlockSpec((P, BC, d_pool), lambda b: (0, b, 0)),
            pl.BlockSpec((C, BC, d_conv), lambda b: (0, b, 0)),
            pl.BlockSpec(gmix.shape, const2),
            pl.BlockSpec(win.shape, const2),
            pl.BlockSpec(wpool_bd.shape, const2),
            pl.BlockSpec(pscale.shape, const2),
            pl.BlockSpec(wdw.shape, const2),
            pl.BlockSpec(bdw.shape, const2),
            pl.BlockSpec(lng.shape, const2),
            pl.BlockSpec(lnb.shape, const2),
            pl.BlockSpec(wout.shape, const2),
        ],
        out_specs=[
            pl.BlockSpec((Ts, BC, D), lambda b: (0, b, 0)),
            pl.BlockSpec((P, BC, d_pool), lambda b: (0, b, 0)),
            pl.BlockSpec((C, BC, d_conv), lambda b: (0, b, 0)),
        ],
        out_shape=[
            jax.ShapeDtypeStruct((Ts, B, D), F32),
            jax.ShapeDtypeStruct((P, B, d_pool), F32),
            jax.ShapeDtypeStruct((C, B, d_conv), F32),
        ],
        compiler_params=pltpu.CompilerParams(
            dimension_semantics=("arbitrary",), vmem_limit_bytes=VMEM_LIMIT),
        name="mixer_sample",
    )(x_tm, pool_tm, conv_tm, gmix, win, wpool_bd, pscale, wdw, bdw, lng, lnb, wout)


def _batcher_pairs(n):
    pairs = []
    t = (n - 1).bit_length()
    p = 1 << (t - 1)
    while p > 0:
        q = 1 << (t - 1)
        r = 0
        d = p
        while d > 0:
            for i in range(n - d):
                if (i & p) == r:
                    pairs.append((i, i + d))
            d = q - p
            q >>= 1
            r = p
        p >>= 1
    return pairs


_SORT16 = _batcher_pairs(TOPK)


def _top16_sorted(s):
    n = s.shape[0] // SUBLANES
    v = [s[k * SUBLANES:(k + 1) * SUBLANES, :] for k in range(n)]
    for i, j in _SORT16:
        hi = jnp.maximum(v[i], v[j])
        lo = jnp.minimum(v[i], v[j])
        v[i], v[j] = hi, lo
    for sh in (4, 2, 1):
        b = [pltpu.roll(x, sh, 0) for x in v]
        v = [jnp.maximum(v[k], b[n - 1 - k]) for k in range(n)]
        st = n // 2
        while st >= 1:
            for k in range(n):
                if not (k & st):
                    hi = jnp.maximum(v[k], v[k + st])
                    lo = jnp.minimum(v[k], v[k + st])
                    v[k], v[k + st] = hi, lo
            st //= 2
    return v


def _kth_largest(cands, k):
    def tree_max(xs):
        xs = list(xs)
        while len(xs) > 1:
            nxt = [jnp.maximum(xs[i], xs[i + 1]) for i in range(0, len(xs) - 1, 2)]
            if len(xs) % 2:
                nxt.append(xs[-1])
            xs = nxt
        return xs[0]

    cur = list(cands)
    for _ in range(k - 1):
        m = tree_max(cur)
        cur = [jnp.where(c == m, -jnp.inf, c) for c in cur]
    return tree_max(cur)


def _peer_kernel(h_ref, gffn_ref, wqT_ref, keys_ref, u_ref, vT_ref, out_ref,
                 xnT_ref, accT_ref, sa_ref, sb_ref, ca_ref, cb_ref, s0_ref, s1_ref, r1_ref, e1_ref, lfp_ref, e0p_ref,
                 sv0_ref, sv1_ref, lcnt_ref, m0_ref, m1_ref, zinv_ref,
                 *, T, E, W, heads):
    j = pl.program_id(1)
    n_pairs = pl.num_programs(1) - 1
    n_keys = keys_ref.shape[1]
    d_half = keys_ref.shape[2]
    n_tiles = T // LANES
    i_per_step = E // n_keys
    n_lt = T // W
    assert i_per_step == SUBLANES and n_keys == 8 * PACK

    def xu_tile(half, st_ref, i, lt):
        ls = slice(lt * W, (lt + 1) * W)
        lo = half * E + i * n_keys
        rows = slice(lo, lo + n_keys) if isinstance(lo, int) else pl.ds(pl.multiple_of(lo, n_keys), n_keys)
        st_ref[i, :, ls] = _dot(u_ref[rows, :], xnT_ref[:, ls])

    def first_step():
        xn = _rmsnorm(h_ref[...], gffn_ref[...])
        xnT = xn.T.astype(BF16)
        xnT_ref[...] = xnT
        accT_ref[...] = jnp.zeros(accT_ref.shape, F32)
        for hd in range(heads):
            qT = _dot(wqT_ref[hd * 2 * d_half:(hd + 1) * 2 * d_half, :], xnT).astype(BF16)
            s0_ref[hd] = _dot(keys_ref[2 * hd], qT[:d_half, :])
            s1_ref[hd] = _dot(keys_ref[2 * hd + 1], qT[d_half:, :])

        sub = lax.broadcasted_iota(jnp.int32, (SUBLANES, LANES), 0)
        pq = [(p, q) for p in range(TOPK) for q in range(TOPK) if (p + 1) * (q + 1) <= TOPK]
        xu_per_tile = i_per_step // n_tiles

        def tile_body(c, carry):
            sl = pl.ds(pl.multiple_of(c * LANES, LANES), LANES)
            xu_queue = [functools.partial(xu_tile, half, st, c * xu_per_tile + k, lt)
                        for half, st in ((0, sa_ref), (1, sb_ref)) for lt in range(n_lt) for k in range(xu_per_tile)]
            sv0 = [jnp.zeros((SUBLANES, LANES), F32)] * TOPK
            sv1 = [jnp.zeros((SUBLANES, LANES), F32)] * TOPK
            for hd in range(heads):
                if xu_queue:
                    xu_queue.pop(0)()
                t0 = _top16_sorted(s0_ref[hd, :, sl])
                t1 = _top16_sorted(s1_ref[hd, :, sl])
                sv0 = [jnp.where(sub == hd, t0[k], sv0[k]) for k in range(TOPK)]
                sv1 = [jnp.where(sub == hd, t1[k], sv1[k]) for k in range(TOPK)]
            cands = [sv0[p] + sv1[q] for p, q in pq]
            thr = _kth_largest(cands, TOPK)
            m0 = sv0[0]
            m1 = sv1[0]
            ex0 = [jnp.exp(x - m0) for x in sv0]
            ex1 = [jnp.exp(x - m1) for x in sv1]
            z = jnp.zeros((SUBLANES, LANES), F32)
            cnt = [jnp.zeros((SUBLANES, LANES), F32)] * TOPK
            for (p, q), cnd in zip(pq, cands):
                sel = cnd >= thr
                z = z + jnp.where(sel, ex0[p] * ex1[q], 0.0)
                cnt[p] = cnt[p] + jnp.where(sel, 1.0, 0.0)
            for k in range(TOPK):
                sv0_ref[k, :, sl] = sv0[k]
                sv1_ref[k, :, sl] = sv1[k]
                lcnt_ref[k, :, sl] = cnt[k]
            m0_ref[:, sl] = m0
            m1_ref[:, sl] = m1
            zinv_ref[:, sl] = 1.0 / z

            for hd in range(heads):
                if xu_queue:
                    xu_queue.pop(0)()
                hrow = slice(hd, hd + 1)
                s0t = s0_ref[hd, :, sl]
                s1t = s1_ref[hd, :, sl]
                lf = jnp.broadcast_to(lcnt_ref[0, hrow, sl], s0t.shape)
                r1 = jnp.zeros(s1t.shape, F32)
                for k in range(TOPK):
                    nxt = lcnt_ref[k + 1, hrow, sl] if k + 1 < TOPK else 0.0
                    lf = jnp.where(sv0_ref[k, hrow, sl] > s0t, nxt, lf)
                    r1 = jnp.where(sv1_ref[k, hrow, sl] > s1t, float(k + 1), r1)
                e0 = jnp.exp(s0t - m0_ref[hrow, sl]) * zinv_ref[hrow, sl]
                e1 = jnp.exp(s1t - m1_ref[hrow, sl])
                r1b = r1.astype(BF16)
                e1b = e1.astype(BF16)
                for g in range(n_keys // PACK):
                    r1_ref[hd, g, :, sl] = r1b[g * PACK:(g + 1) * PACK, :]
                    e1_ref[hd, g, :, sl] = e1b[g * PACK:(g + 1) * PACK, :]
                lfp_ref[hd, c] = lf.reshape(n_keys // SUBLANES, SUBLANES, LANES)
                e0p_ref[hd, c] = e0.reshape(n_keys // SUBLANES, SUBLANES, LANES)
            while xu_queue:
                xu_queue.pop(0)()
            return carry

        assert i_per_step % n_tiles == 0
        lax.fori_loop(0, n_tiles, tile_body, 0)

    def bcast_row(ref, hd, lt, blk, i):
        parts = [jnp.broadcast_to(ref[hd, lt * (W // LANES) + k, blk][i:i + 1, :], (PACK, LANES)).astype(BF16)
                 for k in range(W // LANES)]
        return jnp.concatenate(parts, axis=-1)[None]

    zero_b = jnp.zeros((), BF16)

    def gate_tile(blk, st_ref, coef_ref, i, lt):
        ls = slice(lt * W, (lt + 1) * W)
        gsum = jnp.zeros((n_keys // PACK, PACK, W), BF16)
        for hd in range(heads):
            mask = r1_ref[hd, :, :, ls] < bcast_row(lfp_ref, hd, lt, blk, i)
            gsum = gsum + jnp.where(mask, e1_ref[hd, :, :, ls], zero_b) * bcast_row(e0p_ref, hd, lt, blk, i)
        hval = _gelu_tanh_bf16(st_ref[i, :, ls])
        for g in range(n_keys // PACK):
            lo = i * n_keys + g * PACK
            coef_ref[lo:lo + PACK, ls] = hval[g * PACK:(g + 1) * PACK, :] * gsum[g]

    def v_tile(half, coef_ref, lt):
        ls = slice(lt * W, (lt + 1) * W)
        accT_ref[:, ls] += _dot(vT_ref[:, half * E:(half + 1) * E], coef_ref[:, ls])

    def phase(xu=None, gate=None, vmm=None):
        for lt in range(n_lt):
            for i in range(i_per_step):
                if vmm is not None and i == 0:
                    v_tile(*vmm, lt)
                if xu is not None:
                    xu_tile(*xu, i, lt)
                if gate is not None:
                    gate_tile(*gate, i, lt)

    @pl.when(j == 0)
    def _first():
        first_step()
        phase(gate=(0, sa_ref, ca_ref))

    @pl.when(jnp.logical_and(j > 0, j < n_pairs))
    def _steady():
        phase(xu=(0, sa_ref), gate=(2 * j - 1, sb_ref, cb_ref), vmm=(0, ca_ref))
        phase(xu=(1, sb_ref), gate=(2 * j, sa_ref, ca_ref), vmm=(1, cb_ref))

    @pl.when(j == n_pairs)
    def _last():
        phase(gate=(2 * j - 1, sb_ref, cb_ref), vmm=(0, ca_ref))
        phase(vmm=(1, cb_ref))
        out_ref[...] = h_ref[...] + accT_ref[...].T


def _peer(h, gffn, wqT, keys, u_bf, vT_bf, *, T, E, W):
    N, D = h.shape
    heads = keys.shape[0] // 2
    n_keys = keys.shape[1]
    n_exp = u_bf.shape[0]
    n_pairs = n_exp // (2 * E)
    once = pl.Buffered(1)
    kern = functools.partial(_peer_kernel, T=T, E=E, W=W, heads=heads)
    return pl.pallas_call(
        kern,
        grid=(N // T, n_pairs + 1),
        in_specs=[
            pl.BlockSpec((T, D), lambda t, j: (t, 0)),
            pl.BlockSpec(gffn.shape, lambda t, j: (0, 0)),
            pl.BlockSpec(wqT.shape, lambda t, j: (0, 0), pipeline_mode=once),
            pl.BlockSpec(keys.shape, lambda t, j: (0, 0, 0), pipeline_mode=once),
            pl.BlockSpec((2 * E, D), lambda t, j: (jnp.minimum(j, n_pairs - 1), 0)),
            pl.BlockSpec((D, 2 * E), lambda t, j: (0, jnp.maximum(j - 1, 0))),
        ],
        out_specs=pl.BlockSpec((T, D), lambda t, j: (t, 0)),
        out_shape=jax.ShapeDtypeStruct((N, D), F32),
        scratch_shapes=[
            pltpu.VMEM((D, T), BF16),
            pltpu.VMEM((D, T), F32),
            pltpu.VMEM((E // n_keys, n_keys, T), F32),
            pltpu.VMEM((E // n_keys, n_keys, T), F32),
            pltpu.VMEM((E, T), BF16),
            pltpu.VMEM((E, T), BF16),
            pltpu.VMEM((heads, n_keys, T), F32),
            pltpu.VMEM((heads, n_keys, T), F32),
            pltpu.VMEM((heads, n_keys // PACK, PACK, T), BF16),
            pltpu.VMEM((heads, n_keys // PACK, PACK, T), BF16),
            pltpu.VMEM((heads, T // LANES, n_keys // SUBLANES, SUBLANES, LANES), F32),
            pltpu.VMEM((heads, T // LANES, n_keys // SUBLANES, SUBLANES, LANES), F32),
            pltpu.VMEM((TOPK, SUBLANES, T), F32),
            pltpu.VMEM((TOPK, SUBLANES, T), F32),
            pltpu.VMEM((TOPK, SUBLANES, T), F32),
            pltpu.VMEM((SUBLANES, T), F32),
            pltpu.VMEM((SUBLANES, T), F32),
            pltpu.VMEM((SUBLANES, T), F32),
        ],
        compiler_params=pltpu.CompilerParams(
            dimension_semantics=("arbitrary", "arbitrary"), vmem_limit_bytes=VMEM_LIMIT),
        name="peer",
    )(h, gffn, wqT, keys, u_bf, vT_bf)


def _ple_kernel(h_ref, p_ref, gple_ref, wgate_ref, wproj_ref, gfin_ref, out_ref, *, final):
    h = h_ref[...]
    hn = _rmsnorm(h, gple_ref[...]).astype(BF16)
    gate = _sigmoid(_dot(hn, wgate_ref[...]))
    proj = _dot(p_ref[...].astype(BF16), wproj_ref[...])
    h = h + proj * gate
    if final:
        h = _rmsnorm(h, gfin_ref[...])
    out_ref[...] = h


def _ple(h, p, gple, wgate, wproj, gfin, *, T, final):
    N, D = h.shape
    dp = p.shape[1]
    const2 = lambda t: (0, 0)
    return pl.pallas_call(
        functools.partial(_ple_kernel, final=final),
        grid=(N // T,),
        in_specs=[
            pl.BlockSpec((T, D), lambda t: (t, 0)),
            pl.BlockSpec((T, dp), lambda t: (t, 0)),
            pl.BlockSpec(gple.shape, const2),
            pl.BlockSpec(wgate.shape, const2),
            pl.BlockSpec(wproj.shape, const2),
            pl.BlockSpec(gfin.shape, const2),
        ],
        out_specs=pl.BlockSpec((T, D), lambda t: (t, 0)),
        out_shape=jax.ShapeDtypeStruct((N, D), F32),
        compiler_params=pltpu.CompilerParams(
            dimension_semantics=("arbitrary",), vmem_limit_bytes=VMEM_LIMIT),
        name="ple",
    )(h, p, gple, wgate, wproj, gfin)


def _transpose_cast_kernel(x_ref, o_ref):
    o_ref[...] = x_ref[...].T.astype(BF16)


def _transpose_cast(x, *, R):
    n, d = x.shape
    return pl.pallas_call(
        _transpose_cast_kernel,
        grid=(n // R,),
        in_specs=[pl.BlockSpec((R, d), lambda r: (r, 0))],
        out_specs=pl.BlockSpec((d, R), lambda r: (0, r)),
        out_shape=jax.ShapeDtypeStruct((d, n), BF16),
        compiler_params=pltpu.CompilerParams(
            dimension_semantics=("arbitrary",), vmem_limit_bytes=VMEM_LIMIT),
        name="v_transpose_cast",
    )(x)


def _block_diag(ws):
    g, n, _ = ws.shape
    out = jnp.zeros((g * n, g * n), ws.dtype)
    for k in range(g):
        out = out.at[k * n:(k + 1) * n, k * n:(k + 1) * n].set(ws[k])
    return out


def kernel(x_prompt, x_sample, state_pool, state_conv, p_prompt, p_sample, g_mix, w_in, w_pool, pool_scale, w_dw, b_dw, ln_g, ln_b, w_out, g_ffn, w_q, sub_keys, u_tab, v_tab, g_ple, w_ple_gate, w_ple_proj, g_final):
    depth = g_mix.shape[0]
    B, S, D = x_prompt.shape
    Bs, Ts, _ = x_sample.shape
    row = lambda v: v.reshape(1, -1)

    T_MIX = min(512, S)
    T_PEER = 512
    E_BLK = 1024
    W_GATE = 256
    BC = min(64, Bs)

    hp = x_prompt
    hs = jnp.transpose(x_sample, (1, 0, 2))
    pool_p, conv_p, pool_s, conv_s = [], [], [], []
    for i in range(depth):
        win = w_in[i].astype(BF16)
        wpool_bd = _block_diag(w_pool[i]).astype(BF16)
        wout = w_out[i].astype(BF16)
        mix_w = (row(g_mix[i]), win, wpool_bd, row(pool_scale[i]), w_dw[i], row(b_dw[i]), row(ln_g[i]),
                 row(ln_b[i]), wout)
        wqT = w_q[i].T.astype(BF16)
        heads, _, n_keys, d_half = sub_keys[i].shape
        keys = sub_keys[i].reshape(heads * 2, n_keys, d_half).astype(BF16)
        u_bf = u_tab[i].astype(BF16)
        vT_bf = _transpose_cast(v_tab[i], R=E_BLK)
        wgate = w_ple_gate[i].astype(BF16)
        wproj = w_ple_proj[i].astype(BF16)
        final = i == depth - 1

        hp, npool, nconv = _mixer_prompt(hp, *mix_w, T=T_MIX)
        pool_p.append(npool[0])
        conv_p.append(nconv[0])
        hs, spool, sconv = _mixer_sample(
            hs, jnp.transpose(state_pool[i], (1, 0, 2)), jnp.transpose(state_conv[i], (1, 0, 2)), *mix_w, BC=BC)
        pool_s.append(jnp.transpose(spool, (1, 0, 2)))
        conv_s.append(jnp.transpose(sconv, (1, 0, 2)))

        hp2 = _peer(hp.reshape(B * S, D), row(g_ffn[i]), wqT, keys, u_bf, vT_bf, T=T_PEER, E=E_BLK, W=W_GATE)
        hs2 = _peer(hs.reshape(Ts * Bs, D), row(g_ffn[i]), wqT, keys, u_bf, vT_bf,
                    T=min(T_PEER, Ts * Bs), E=E_BLK, W=W_GATE)
        pp = p_prompt[i].reshape(B * S, -1)
        ps = jnp.transpose(p_sample[i], (1, 0, 2)).reshape(Ts * Bs, -1)
        hp = _ple(hp2, pp, row(g_ple[i]), wgate, wproj, row(g_final), T=T_PEER, final=final).reshape(B, S, D)
        hs = _ple(hs2, ps, row(g_ple[i]), wgate, wproj, row(g_final), T=min(T_PEER, Ts * Bs),
                  final=final).reshape(Ts, Bs, D)

    y_prompt = hp
    y_sample = jnp.transpose(hs, (1, 0, 2))
    return (y_prompt, y_sample, jnp.stack(pool_p), jnp.stack(conv_p), jnp.stack(pool_s), jnp.stack(conv_s))
```
